```python
import math
import jax
import jax.numpy as jnp
from jax import lax
import numpy as np

D_MODEL = 1024
BATCH = 8
SEQ = 2048
DEPTH = 4
DEC_BATCH = 128
DEC_SEQ = 4
PAST_LEN = 2048
PAGE_SIZE = 128

N_A_LAYERS = DEPTH // 2
N_B_LAYERS = DEPTH - N_A_LAYERS
N_DENSE = (DEPTH + 1) // 2
N_MOE = DEPTH // 2
N_PAGES = PAST_LEN // PAGE_SIZE

SSM_EXPAND = 2
D_INNER = SSM_EXPAND * D_MODEL
SSM_HEAD_DIM = 64
SSM_HEADS = D_INNER // SSM_HEAD_DIM
SSM_GROUPS = 8
HEADS_PER_GROUP = SSM_HEADS // SSM_GROUPS
SSM_STATE = 128
CONV_WIDTH = 4
CONV_DIM = D_INNER + 2 * SSM_GROUPS * SSM_STATE
IN_PROJ_DIM = D_INNER + CONV_DIM + SSM_HEADS
SSD_CHUNK = 128
ATT_HEAD_DIM = 64
ATT_HEADS = D_MODEL // (2 * ATT_HEAD_DIM)
ATT_V_DIM = 2 * ATT_HEAD_DIM
Q_BLOCK = 128
D_FF = 2816
N_EXPERTS = 8
TOP_K = 2
MOE_D_FF = 2816
RMS_EPS = 1e-5

kernel_name = 'yoco_mamba2_diffattn_step'


def rmsnorm(x, g):
    xf = x.astype(jnp.float32)
    xf = xf * lax.rsqrt(jnp.mean(xf * xf, axis=-1, keepdims=True) + RMS_EPS)
    return (xf * g.astype(jnp.float32)).astype(x.dtype)


def swiglu(x, w_gate, w_up, w_down):
    return (jax.nn.silu(x @ w_gate) * (x @ w_up)) @ w_down


def moe_swiglu(x, w_router, b_router, w_gate, w_up, w_down):
    logits = (x @ w_router).astype(jnp.float32) + b_router.astype(jnp.float32)
    top_v, top_i = lax.top_k(logits, TOP_K)
    gates = jax.nn.softmax(top_v, axis=-1)
    combine = jnp.sum(jax.nn.one_hot(top_i, N_EXPERTS, dtype=jnp.float32) * gates[..., None], axis=-2)
    out = jnp.zeros_like(x)
    for e in range(N_EXPERTS):
        out = out + combine[..., e:e + 1].astype(x.dtype) * swiglu(x, w_gate[e], w_up[e], w_down[e])
    return out


def lambda_init_fn(layer):
    return 0.8 - 0.6 * math.exp(-0.3 * layer)


def alibi_slopes():
    return 2.0 ** (-8.0 * jnp.arange(1, ATT_HEADS + 1, dtype=jnp.float32) / ATT_HEADS)


def ssd_scan(x, dt, a, bm, cm, s0):
    b, l = x.shape[:2]
    q = math.gcd(l, SSD_CHUNK)
    c = l // q
    x = x.reshape(b, c, q, SSM_GROUPS, HEADS_PER_GROUP, SSM_HEAD_DIM)
    dt = dt.reshape(b, c, q, SSM_GROUPS, HEADS_PER_GROUP)
    bm = bm.reshape(b, c, q, SSM_GROUPS, SSM_STATE)
    cm = cm.reshape(b, c, q, SSM_GROUPS, SSM_STATE)
    cs = jnp.cumsum(dt * a, axis=2)
    xdt = x * dt[..., None]
    causal = jnp.tril(jnp.ones((q, q), dtype=bool))[:, :, None, None]
    seg = cs[:, :, :, None] - cs[:, :, None, :]
    decay_in = jnp.exp(jnp.where(causal, seg, -jnp.inf))
    cb = jnp.einsum('bcign,bcjgn->bcijg', cm, bm)
    y_diag = jnp.einsum('bcijgr,bcjgrp->bcigrp', cb[..., None] * decay_in, xdt)
    decay_out = jnp.exp(cs[:, :, -1:] - cs)
    chunk_states = jnp.einsum('bcjgn,bcjgrp->bcgrpn', bm, xdt * decay_out[..., None])
    chunk_decay = jnp.exp(cs[:, :, -1])

    def step(s, inp):
        st, dec = inp
        return s * dec[..., None, None] + st, s

    s_fin, s_prev = lax.scan(step, s0, (jnp.moveaxis(chunk_states, 1, 0), jnp.moveaxis(chunk_decay, 1, 0)))
    s_prev = jnp.moveaxis(s_prev, 0, 1)
    y_off = jnp.einsum('bcign,bcgrpn->bcigrp', cm, s_prev) * jnp.exp(cs)[..., None]
    y = (y_diag + y_off).reshape(b, l, SSM_GROUPS, HEADS_PER_GROUP, SSM_HEAD_DIM)
    return y, s_fin


def mamba2_mixer(h, ssm0, conv0, w_in, conv_w, conv_b, dt_bias, a_log, d_skip, norm_g, w_out):
    b, l, _ = h.shape
    zxbcdt = h @ w_in
    z = zxbcdt[..., :D_INNER]
    xbc = zxbcdt[..., D_INNER:D_INNER + CONV_DIM]
    dt_raw = zxbcdt[..., D_INNER + CONV_DIM:]
    xpad = jnp.concatenate([conv0.astype(xbc.dtype), xbc], axis=1)
    conv = conv_b + xpad[:, 0:l] * conv_w[0]
    for k in range(1, CONV_WIDTH):
        conv = conv + xpad[:, k:k + l] * conv_w[k]
    conv_new = xpad[:, l:]
    xbc = jax.nn.silu(conv)
    gn = SSM_GROUPS * SSM_STATE
    xs = xbc[..., :D_INNER].astype(jnp.float32).reshape(b, l, SSM_GROUPS, HEADS_PER_GROUP, SSM_HEAD_DIM)
    bm = xbc[..., D_INNER:D_INNER + gn].astype(jnp.float32).reshape(b, l, SSM_GROUPS, SSM_STATE)
    cm = xbc[..., D_INNER + gn:].astype(jnp.float32).reshape(b, l, SSM_GROUPS, SSM_STATE)
    dt = jax.nn.softplus(dt_raw.astype(jnp.float32) + dt_bias.astype(jnp.float32)).reshape(b, l, SSM_GROUPS, HEADS_PER_GROUP)
    a = -jnp.exp(a_log.astype(jnp.float32)).reshape(SSM_GROUPS, HEADS_PER_GROUP)
    s0 = ssm0.astype(jnp.float32).reshape(b, SSM_GROUPS, HEADS_PER_GROUP, SSM_HEAD_DIM, SSM_STATE)
    y, s_fin = ssd_scan(xs, dt, a, bm, cm, s0)
    y = y + d_skip.astype(jnp.float32).reshape(SSM_GROUPS, HEADS_PER_GROUP)[..., None] * xs
    y = y.reshape(b, l, D_INNER) * jax.nn.silu(z.astype(jnp.float32))
    yg = y.reshape(b, l, SSM_GROUPS, D_INNER // SSM_GROUPS)
    yg = yg * lax.rsqrt(jnp.mean(yg * yg, axis=-1, keepdims=True) + RMS_EPS)
    y = (yg.reshape(b, l, D_INNER) * norm_g.astype(jnp.float32)).astype(h.dtype)
    s_fin = s_fin.reshape(b, SSM_HEADS, SSM_HEAD_DIM, SSM_STATE).astype(ssm0.dtype)
    return y @ w_out, s_fin, conv_new.astype(conv0.dtype)


def diff_mix(q, segments, q_pos, lam, slopes):
    scale = ATT_HEAD_DIM ** -0.5
    scores = []
    for k, v, k_pos in segments:
        s = jnp.einsum('bqhmd,bkhmd->bhmqk', q, k).astype(jnp.float32) * scale
        dist = (q_pos[:, None] - k_pos[None, :]).astype(jnp.float32)
        bias = jnp.where(dist >= 0, -slopes[:, None, None] * dist, -jnp.inf)
        scores.append(s + bias[:, None])
    p = jax.nn.softmax(jnp.concatenate(scores, axis=-1), axis=-1)
    a = p[:, :, 0] - lam * p[:, :, 1]
    out = None
    off = 0
    for k, v, _ in segments:
        tk = k.shape[1]
        part = jnp.einsum('bhqk,bkhd->bqhd', a[..., off:off + tk].astype(v.dtype), v)
        out = part if out is None else out + part
        off += tk
    return out


def diff_attention(h, k_new, v_new, past, w_q, lq1, lk1, lq2, lk2, subln_g, w_o, lam_init, slopes):
    b, l, _ = h.shape
    q = (h @ w_q).reshape(b, l, ATT_HEADS, 2, ATT_HEAD_DIM)
    k = k_new.reshape(b, l, ATT_HEADS, 2, ATT_HEAD_DIM)
    f32 = jnp.float32
    lam = (jnp.exp(jnp.sum(lq1.astype(f32) * lk1.astype(f32)))
           - jnp.exp(jnp.sum(lq2.astype(f32) * lk2.astype(f32))) + lam_init)
    new_pos = jnp.arange(l, dtype=jnp.int32)
    if past is None:
        qb = min(Q_BLOCK, l)
        nb = l // qb
        qs = jnp.moveaxis(q.reshape(b, nb, qb, ATT_HEADS, 2, ATT_HEAD_DIM), 1, 0)
        starts = jnp.arange(nb, dtype=jnp.int32) * qb
        segs = ((k, v_new, new_pos),)

        def block(args):
            q_blk, start = args
            return diff_mix(q_blk, segs, start + jnp.arange(qb, dtype=jnp.int32), lam, slopes)

        o = lax.map(block, (qs, starts))
        o = jnp.moveaxis(o, 0, 1).reshape(b, l, ATT_HEADS, ATT_V_DIM)
    else:
        k_past, v_past = past
        p_len = k_past.shape[1]
        segs = ((k_past, v_past, jnp.arange(p_len, dtype=jnp.int32)), (k, v_new, p_len + new_pos))
        o = diff_mix(q, segs, p_len + new_pos, lam, slopes)
    o = rmsnorm(o, subln_g) * (1.0 - lam_init)
    return o.reshape(b, l, ATT_HEADS * ATT_V_DIM) @ w_o


def setup_inputs(seed: int = 0) -> dict:
    key = jax.random.key(seed)
    ks = iter(jax.random.split(key, 48))
    f32 = jnp.float32

    def nrm(shape, scale=1.0):
        return jax.random.normal(next(ks), shape, f32) * scale

    def gain(shape):
        return 1.0 + 0.02 * jax.random.normal(next(ks), shape, f32)

    n_used = DEC_BATCH * N_PAGES
    n_pool = n_used + max(1, n_used // 4)
    x_prompt = nrm((BATCH, SEQ, D_MODEL))
    x_sample = nrm((DEC_BATCH, DEC_SEQ, D_MODEL))
    state_ssm = nrm((N_A_LAYERS, DEC_BATCH, SSM_HEADS, SSM_HEAD_DIM, SSM_STATE), 0.5)
    state_conv = nrm((N_A_LAYERS, DEC_BATCH, CONV_WIDTH - 1, CONV_DIM))
    cache_k = nrm((n_pool, PAGE_SIZE, ATT_HEADS, 2 * ATT_HEAD_DIM))
    cache_v = nrm((n_pool, PAGE_SIZE, ATT_HEADS, ATT_V_DIM))
    page_table = jax.random.permutation(next(ks), n_pool)[:n_used].reshape(DEC_BATCH, N_PAGES).astype(jnp.int32)
    dt0 = jnp.exp(jax.random.uniform(next(ks), (N_A_LAYERS, SSM_HEADS), f32, math.log(1e-3), math.log(1e-1)))
    dt_bias = dt0 + jnp.log(-jnp.expm1(-dt0))
    a_log = jnp.log(jax.random.uniform(next(ks), (N_A_LAYERS, SSM_HEADS), f32, 1.0, 16.0))
    return {
        'x_prompt': x_prompt,
        'x_sample': x_sample,
        'state_ssm': state_ssm,
        'state_conv': state_conv,
        'cache_k': cache_k,
        'cache_v': cache_v,
        'page_table': page_table,
        'norm_mix_g': gain((DEPTH, D_MODEL)),
        'norm_ffn_g': gain((DEPTH, D_MODEL)),
        'norm_kv_g': gain((D_MODEL,)),
        'norm_final_g': gain((D_MODEL,)),
        'ssm_w_in': nrm((N_A_LAYERS, D_MODEL, IN_PROJ_DIM), D_MODEL ** -0.5),
        'ssm_conv_w': nrm((N_A_LAYERS, CONV_WIDTH, CONV_DIM), CONV_WIDTH ** -0.5),
        'ssm_conv_b': nrm((N_A_LAYERS, CONV_DIM), 0.02),
        'ssm_dt_bias': dt_bias,
        'ssm_a_log': a_log,
        'ssm_d': gain((N_A_LAYERS, SSM_HEADS)),
        'ssm_norm_g': gain((N_A_LAYERS, D_INNER)),
        'ssm_w_out': nrm((N_A_LAYERS, D_INNER, D_MODEL), D_INNER ** -0.5),
        'kv_w_k': nrm((D_MODEL, ATT_HEADS * 2 * ATT_HEAD_DIM), D_MODEL ** -0.5),
        'kv_w_v': nrm((D_MODEL, ATT_HEADS * ATT_V_DIM), D_MODEL ** -0.5),
        'att_w_q': nrm((N_B_LAYERS, D_MODEL, ATT_HEADS * 2 * ATT_HEAD_DIM), D_MODEL ** -0.5),
        'att_lam_q1': nrm((N_B_LAYERS, ATT_HEAD_DIM), 0.1),
        'att_lam_k1': nrm((N_B_LAYERS, ATT_HEAD_DIM), 0.1),
        'att_lam_q2': nrm((N_B_LAYERS, ATT_HEAD_DIM), 0.1),
        'att_lam_k2': nrm((N_B_LAYERS, ATT_HEAD_DIM), 0.1),
        'att_subln_g': gain((N_B_LAYERS, ATT_V_DIM)),
        'att_w_o': nrm((N_B_LAYERS, ATT_HEADS * ATT_V_DIM, D_MODEL), (ATT_HEADS * ATT_V_DIM) ** -0.5),
        'ffn_w_gate': nrm((N_DENSE, D_MODEL, D_FF), D_MODEL ** -0.5),
        'ffn_w_up': nrm((N_DENSE, D_MODEL, D_FF), D_MODEL ** -0.5),
        'ffn_w_down': nrm((N_DENSE, D_FF, D_MODEL), D_FF ** -0.5),
        'moe_w_router': nrm((N_MOE, D_MODEL, N_EXPERTS), D_MODEL ** -0.5),
        'moe_b_router': nrm((N_MOE, N_EXPERTS), 0.01),
        'moe_w_gate': nrm((N_MOE, N_EXPERTS, D_MODEL, MOE_D_FF), D_MODEL ** -0.5),
        'moe_w_up': nrm((N_MOE, N_EXPERTS, D_MODEL, MOE_D_FF), D_MODEL ** -0.5),
        'moe_w_down': nrm((N_MOE, N_EXPERTS, MOE_D_FF, D_MODEL), MOE_D_FF ** -0.5),
    }


def reference(x_prompt, x_sample, state_ssm, state_conv, cache_k, cache_v, page_table,
              norm_mix_g, norm_ffn_g, norm_kv_g, norm_final_g,
              ssm_w_in, ssm_conv_w, ssm_conv_b, ssm_dt_bias, ssm_a_log, ssm_d, ssm_norm_g, ssm_w_out,
              kv_w_k, kv_w_v,
              att_w_q, att_lam_q1, att_lam_k1, att_lam_q2, att_lam_k2, att_subln_g, att_w_o,
              ffn_w_gate, ffn_w_up, ffn_w_down,
              moe_w_router, moe_b_router, moe_w_gate, moe_w_up, moe_w_down):
    slopes = alibi_slopes()

    def trunk(x, ssm0, conv0, past):
        b, l, _ = x.shape
        ssm_new, conv_new = [], []
        k_new = None
        v_new = None
        for layer in range(DEPTH):
            h = rmsnorm(x, norm_mix_g[layer])
            if layer < N_A_LAYERS:
                y, s, cv = mamba2_mixer(h, ssm0[layer], conv0[layer], ssm_w_in[layer], ssm_conv_w[layer],
                                        ssm_conv_b[layer], ssm_dt_bias[layer], ssm_a_log[layer],
                                        ssm_d[layer], ssm_norm_g[layer], ssm_w_out[layer])
                ssm_new.append(s)
                conv_new.append(cv)
            else:
                j = layer - N_A_LAYERS
                y = diff_attention(h, k_new, v_new, past, att_w_q[j], att_lam_q1[j], att_lam_k1[j],
                                   att_lam_q2[j], att_lam_k2[j], att_subln_g[j], att_w_o[j],
                                   lambda_init_fn(layer), slopes)
            x = x + y
            h = rmsnorm(x, norm_ffn_g[layer])
            i = layer // 2
            if layer % 2 == 0:
                x = x + swiglu(h, ffn_w_gate[i], ffn_w_up[i], ffn_w_down[i])
            else:
                x = x + moe_swiglu(h, moe_w_router[i], moe_b_router[i], moe_w_gate[i], moe_w_up[i], moe_w_down[i])
            if layer == N_A_LAYERS - 1:
                hk = rmsnorm(x, norm_kv_g)
                k_new = (hk @ kv_w_k).reshape(b, l, ATT_HEADS, 2 * ATT_HEAD_DIM)
                v_new = (hk @ kv_w_v).reshape(b, l, ATT_HEADS, ATT_V_DIM)
        return rmsnorm(x, norm_final_g), jnp.stack(ssm_new), jnp.stack(conv_new), k_new, v_new

    bp = x_prompt.shape[0]
    ssm0_p = jnp.zeros((N_A_LAYERS, bp, SSM_HEADS, SSM_HEAD_DIM, SSM_STATE), x_prompt.dtype)
    conv0_p = jnp.zeros((N_A_LAYERS, bp, CONV_WIDTH - 1, CONV_DIM), x_prompt.dtype)
    y_prompt, ssm_p, conv_p, k_p, v_p = trunk(x_prompt, ssm0_p, conv0_p, None)

    bs = page_table.shape[0]
    k_past = cache_k[page_table].reshape(bs, -1, ATT_HEADS, 2, ATT_HEAD_DIM)
    v_past = cache_v[page_table].reshape(bs, -1, ATT_HEADS, ATT_V_DIM)
    y_sample, ssm_s, conv_s, k_s, v_s = trunk(x_sample, state_ssm, state_conv, (k_past, v_past))

    return (y_prompt, y_sample, ssm_p, conv_p, k_p, v_p, ssm_s, conv_s, k_s, v_s)
```

```python
import functools
import math

import jax
import jax.numpy as jnp
from jax import lax
from jax.experimental import pallas as pl
from jax.experimental.pallas import tpu as pltpu

F32 = jnp.float32
BF16 = jnp.bfloat16
I32 = jnp.int32

RMS_EPS = 1e-5
SSM_HEAD_DIM = 64
SSM_GROUPS = 8
SSM_STATE = 128
CONV_WIDTH = 4
SSD_CHUNK = 128
ATT_HEAD_DIM = 64
N_EXPERTS = 8
LANE = 128
SUBLANE = 8
VMEM_LIMIT = 56 * 1024 * 1024
NEG = -1e30

TOKEN_TILE = 256
MOE_SLOT_TILE = 256
MOE_CHUNK = 256
ATT_TILE = 256
PAGES_PER_STEP = 8


def _params(*sem):
    return pltpu.CompilerParams(dimension_semantics=sem, vmem_limit_bytes=VMEM_LIMIT)


def _resident(shape):
    nd = len(shape)
    return pl.BlockSpec(shape, lambda *_: (0,) * nd, pipeline_mode=pl.Buffered(1))


def _rms(x, g):
    ms = jnp.mean(x * x, axis=-1, keepdims=True)
    return x * lax.rsqrt(ms + RMS_EPS) * g


def _silu(x):
    return x * jax.nn.sigmoid(x)


def _dot(a, b):
    return jnp.dot(a, b, preferred_element_type=F32)


def _dot_nt(a, b):
    return lax.dot_general(a, b, (((1,), (1,)), ((), ())), preferred_element_type=F32)


def _dot_tn(a, b):
    return lax.dot_general(a, b, (((0,), (0,)), ((), ())), preferred_element_type=F32)


def _split2(x):
    hi = x.astype(BF16)
    lo = (x - hi.astype(F32)).astype(BF16)
    return hi, lo


def _split3(x):
    hi = x.astype(BF16)
    r = x - hi.astype(F32)
    mid = r.astype(BF16)
    lo = (r - mid.astype(F32)).astype(BF16)
    return hi, mid, lo


def _linear_kernel(*refs, norm, residual):
    it = iter(refs)
    x_ref = next(it)
    g_ref = next(it) if norm else None
    w_ref = next(it)
    r_ref = next(it) if residual else None
    o_ref = next(it)
    x = x_ref[...].astype(F32)
    if norm:
        x = _rms(x, g_ref[...])
    y = _dot(x.astype(BF16), w_ref[...])
    if residual:
        y = y + r_ref[...]
    o_ref[...] = y


def linear(x, w, g=None, residual=None, tm=TOKEN_TILE):
    t, k = x.shape
    n = w.shape[1]
    tm = min(tm, t)
    assert t % tm == 0
    args = [x]
    specs = [pl.BlockSpec((tm, k), lambda i: (i, 0))]
    if g is not None:
        args.append(g.reshape(1, k))
        specs.append(_resident((1, k)))
    args.append(w)
    specs.append(_resident((k, n)))
    if residual is not None:
        args.append(residual)
        specs.append(pl.BlockSpec((tm, n), lambda i: (i, 0)))
    return pl.pallas_call(
        functools.partial(_linear_kernel, norm=g is not None, residual=residual is not None),
        grid=(t // tm,),
        in_specs=specs,
        out_specs=pl.BlockSpec((tm, n), lambda i: (i, 0)),
        out_shape=jax.ShapeDtypeStruct((t, n), F32),
        compiler_params=_params("parallel"),
        name="linear",
    )(*args)


def _ffn_kernel(x_ref, g_ref, wg_ref, wu_ref, wd_ref, o_ref):
    x = x_ref[...]
    h = _rms(x, g_ref[...]).astype(BF16)
    a = _dot(h, wg_ref[...])
    u = _dot(h, wu_ref[...])
    act = (_silu(a) * u).astype(BF16)
    o_ref[...] = x + _dot(act, wd_ref[...])


def ffn(x, g, wg, wu, wd, tm=TOKEN_TILE):
    t, d = x.shape
    f = wg.shape[1]
    tm = min(tm, t)
    assert t % tm == 0
    return pl.pallas_call(
        _ffn_kernel,
        grid=(t // tm,),
        in_specs=[pl.BlockSpec((tm, d), lambda i: (i, 0)), _resident((1, d)),
                  _resident((d, f)), _resident((d, f)), _resident((f, d))],
        out_specs=pl.BlockSpec((tm, d), lambda i: (i, 0)),
        out_shape=jax.ShapeDtypeStruct((t, d), F32),
        compiler_params=_params("parallel"),
        name="ffn",
    )(x, g.reshape(1, d), wg, wu, wd)


def _rmsnorm_kernel(x_ref, g_ref, o_ref):
    o_ref[...] = _rms(x_ref[...], g_ref[...])


def rmsnorm(x, g, tm=TOKEN_TILE):
    t, d = x.shape
    tm = min(tm, t)
    assert t % tm == 0
    return pl.pallas_call(
        _rmsnorm_kernel,
        grid=(t // tm,),
        in_specs=[pl.BlockSpec((tm, d), lambda i: (i, 0)), _resident((1, d))],
        out_specs=pl.BlockSpec((tm, d), lambda i: (i, 0)),
        out_shape=jax.ShapeDtypeStruct((t, d), F32),
        compiler_params=_params("parallel"),
        name="rmsnorm",
    )(x, g.reshape(1, d))


def _ssd_kernel(zx_ref, conv0_ref, ssm0_ref, cw_ref, cb_ref, dtb_ref, alog_ref, dsk_ref,
                ng_ref, exp_ref, y_ref, sfin_ref, cnew_ref, xp_ref, stage_ref,
                *, q, rows, valid, d_inner, heads):
    c = pl.program_id(1)
    nc = pl.num_programs(1)
    kp = SSD_CHUNK
    gn = SSM_GROUPS * SSM_STATE
    hpg = heads // SSM_GROUPS
    gw = hpg * SSM_HEAD_DIM

    @pl.when(c == 0)
    def _():
        sfin_ref[...] = ssm0_ref[...]
        xp_ref[0:SUBLANE, :] = jnp.zeros((SUBLANE, xp_ref.shape[1]), F32)
        xp_ref[SUBLANE - (CONV_WIDTH - 1):SUBLANE, :] = conv0_ref[...]

    if rows < q:
        stage_ref[...] = jnp.zeros(stage_ref.shape, F32)
        stage_ref[0:rows, :] = zx_ref[...]
        zx = stage_ref[...]
    else:
        zx = zx_ref[...]
    z = zx[:, :d_inner]
    xbc = zx[:, d_inner:2 * d_inner + 2 * gn]
    dtr = zx[:, 2 * d_inner + 2 * gn:]

    xp_ref[SUBLANE:SUBLANE + q, :] = xbc
    base = SUBLANE - (CONV_WIDTH - 1)
    conv = cb_ref[...] + xp_ref[base:base + q, :] * cw_ref[0:1, :]
    for k in range(1, CONV_WIDTH):
        conv = conv + xp_ref[base + k:base + k + q, :] * cw_ref[k:k + 1, :]
    tail = xp_ref[SUBLANE + valid - (CONV_WIDTH - 1):SUBLANE + valid, :]
    xp_ref[base:SUBLANE, :] = tail

    @pl.when(c == nc - 1)
    def _():
        cnew_ref[...] = tail

    xa = _silu(conv)
    xs = xa[:, :d_inner]
    bm = xa[:, d_inner:d_inner + gn]
    cm = xa[:, d_inner + gn:]

    def pad(v):
        if q == kp:
            return v
        return jnp.concatenate([v, jnp.zeros((kp - q, v.shape[1]), v.dtype)], axis=0)

    row = lax.broadcasted_iota(I32, (q, kp), 0)
    col = lax.broadcasted_iota(I32, (q, kp), 1)
    causal = col <= row

    dtv = dtr + dtb_ref[...]
    dt = jnp.maximum(dtv, 0.0) + jnp.log1p(jnp.exp(-jnp.abs(dtv)))
    if valid < q:
        dt = jnp.where(lax.broadcasted_iota(I32, (q, LANE), 0) < valid, dt, 0.0)
    adt = dt * (-jnp.exp(alog_ref[...]))
    tril = causal.astype(BF16)
    cs = sum(_dot(tril, part) for part in _split3(pad(adt)))
    cs_t = jnp.transpose(pad(cs))
    cs_last = cs[q - 1:q, :]
    e_last = jnp.exp(cs_last)

    stacked = jnp.concatenate([dt, jnp.exp(cs), jnp.exp(cs_last - cs)], axis=0)
    hi, lo = _split2(stacked)
    wide = _dot(hi, exp_ref[...]) + _dot(lo, exp_ref[...])
    dt_w, ecs_w, dout_w = wide[0:q], wide[q:2 * q], wide[2 * q:3 * q]
    xdt = xs * dt_w
    xdtd = pad(xdt * dout_w).astype(BF16)
    xdt_p = pad(xdt)
    bm_p = pad(bm).astype(BF16)
    cm_b = cm.astype(BF16)
    lane_g = lax.broadcasted_iota(I32, (kp, gw), 1) // SSM_HEAD_DIM

    for g in range(SSM_GROUPS):
        b_g = bm_p[:, g * SSM_STATE:(g + 1) * SSM_STATE]
        c_g = cm_b[:, g * SSM_STATE:(g + 1) * SSM_STATE]
        cb = _dot_nt(c_g, b_g)
        xg = xdt_p[:, g * gw:(g + 1) * gw]
        m_parts, x_parts, scale_parts = [], [], []
        for r in range(hpg):
            h = g * hpg + r
            seg = cs[:, h:h + 1] - cs_t[h:h + 1, :]
            decay = jnp.exp(jnp.where(causal, seg, NEG))
            m_parts.append((cb * decay).astype(BF16))
            x_parts.append(jnp.where(lane_g == r, xg, 0.0).astype(BF16))
            scale_parts.append(jnp.broadcast_to(e_last[:, h:h + 1], (SSM_HEAD_DIM, SSM_STATE)))
        y_diag = _dot(jnp.concatenate(m_parts, axis=1), jnp.concatenate(x_parts, axis=0))
        s_g = sfin_ref[g * gw:(g + 1) * gw, :]
        y_off = _dot_nt(c_g, s_g.astype(BF16)) * ecs_w[:, g * gw:(g + 1) * gw]
        new = _dot_tn(xdtd[:, g * gw:(g + 1) * gw], b_g)
        sfin_ref[g * gw:(g + 1) * gw, :] = s_g * jnp.concatenate(scale_parts, axis=0) + new
        y = y_diag + y_off + dsk_ref[:, g * gw:(g + 1) * gw] * xs[:, g * gw:(g + 1) * gw]
        y = y * _silu(z[:, g * gw:(g + 1) * gw])
        y = y * lax.rsqrt(jnp.mean(y * y, axis=-1, keepdims=True) + RMS_EPS)
        y_ref[:, g * gw:(g + 1) * gw] = y * ng_ref[:, g * gw:(g + 1) * gw]


def ssd_mixer(zx, conv0, ssm0, conv_w, conv_b, dt_bias, a_log, d_skip, norm_g, valid):
    b, l, width = zx.shape
    heads = a_log.shape[0]
    d_inner = heads * SSM_HEAD_DIM
    conv_dim = conv_w.shape[1]
    assert width == d_inner + conv_dim + LANE
    if l % SSD_CHUNK == 0:
        q = rows = SSD_CHUNK
        nc = l // SSD_CHUNK
        assert valid == SSD_CHUNK
    else:
        assert l < SSD_CHUNK
        rows, nc = l, 1
        q = -(-l // SUBLANE) * SUBLANE
    padh = LANE - heads
    expand = jnp.repeat(jnp.eye(LANE, heads, dtype=BF16), SSM_HEAD_DIM, axis=1)
    kern = functools.partial(_ssd_kernel, q=q, rows=rows, valid=valid, d_inner=d_inner, heads=heads)
    y, sfin, cnew = pl.pallas_call(
        kern,
        grid=(b, nc),
        in_specs=[
            pl.BlockSpec((None, rows, width), lambda i, j: (i, j, 0)),
            pl.BlockSpec((None, CONV_WIDTH - 1, conv_dim), lambda i, j: (i, 0, 0)),
            pl.BlockSpec((None, d_inner, SSM_STATE), lambda i, j: (i, 0, 0)),
            _resident((CONV_WIDTH, conv_dim)), _resident((1, conv_dim)),
            _resident((1, LANE)), _resident((1, LANE)),
            _resident((1, d_inner)), _resident((1, d_inner)), _resident((LANE, d_inner)),
        ],
        out_specs=[
            pl.BlockSpec((None, q, d_inner), lambda i, j: (i, j, 0)),
            pl.BlockSpec((None, d_inner, SSM_STATE), lambda i, j: (i, 0, 0)),
            pl.BlockSpec((None, CONV_WIDTH - 1, conv_dim), lambda i, j: (i, 0, 0)),
        ],
        out_shape=[
            jax.ShapeDtypeStruct((b, nc * q, d_inner), F32),
            jax.ShapeDtypeStruct((b, d_inner, SSM_STATE), F32),
            jax.ShapeDtypeStruct((b, CONV_WIDTH - 1, conv_dim), F32),
        ],
        scratch_shapes=[pltpu.VMEM((SUBLANE + q, conv_dim), F32),
                        pltpu.VMEM((q, width) if rows < q else (SUBLANE, LANE), F32)],
        compiler_params=_params("parallel", "arbitrary"),
        name="ssd_mixer",
    )(zx, conv0, ssm0, conv_w, conv_b.reshape(1, conv_dim),
      jnp.pad(dt_bias, (0, padh)).reshape(1, LANE), jnp.pad(a_log, (0, padh)).reshape(1, LANE),
      jnp.repeat(d_skip, SSM_HEAD_DIM).reshape(1, d_inner), norm_g.reshape(1, d_inner), expand)
    return y[:, :l], sfin, cnew


def _lambda(lam_ref, lam_init):
    lv = lam_ref[...]
    s1 = jnp.sum(lv[0:1] * lv[1:2], axis=-1, keepdims=True)
    s2 = jnp.sum(lv[2:3] * lv[3:4], axis=-1, keepdims=True)
    return jnp.exp(s1) - jnp.exp(s2) + lam_init


def _two_branch_q(qv, n):
    lane = lax.broadcasted_iota(I32, qv.shape, 1)
    return jnp.concatenate([jnp.where(lane < ATT_HEAD_DIM, qv, 0.0),
                            jnp.where(lane >= ATT_HEAD_DIM, qv, 0.0)], axis=0).astype(BF16)


def _attn_prompt_kernel(slope_ref, q_ref, k_ref, v_ref, lam_ref, sg_ref, o_ref, *, tq, lam_init):
    h = pl.program_id(1)
    qi = pl.program_id(2)
    slope = slope_ref[h]
    q2 = _two_branch_q(q_ref[...] * (ATT_HEAD_DIM ** -0.5), tq)
    r2 = lax.broadcasted_iota(I32, (2 * tq, tq), 0)
    rel = (jnp.where(r2 >= tq, r2 - tq, r2) - lax.broadcasted_iota(I32, (2 * tq, tq), 1)).astype(F32)

    def body(j, carry):
        m, l, acc = carry
        start = pl.multiple_of(j * tq, tq)
        kb = k_ref[pl.ds(start, tq), :].astype(BF16)
        vb = v_ref[pl.ds(start, tq), :].astype(BF16)
        dist = rel + ((qi - j) * tq).astype(F32)
        s = _dot_nt(q2, kb) + jnp.where(dist >= 0, -slope * dist, NEG)
        m_new = jnp.maximum(m, jnp.max(s, axis=-1, keepdims=True))
        alpha = jnp.exp(m - m_new)
        p = jnp.exp(s - m_new)
        l = alpha * l + jnp.sum(p, axis=-1, keepdims=True)
        acc = alpha * acc + _dot(p.astype(BF16), vb)
        return m_new, l, acc

    m0 = jnp.full((2 * tq, 1), NEG, F32)
    l0 = jnp.zeros((2 * tq, 1), F32)
    a0 = jnp.zeros((2 * tq, v_ref.shape[1]), F32)
    _, l, acc = lax.fori_loop(0, qi + 1, body, (m0, l0, a0))
    o = acc / l
    o = o[:tq] - _lambda(lam_ref, lam_init) * o[tq:]
    o_ref[...] = _rms(o, sg_ref[...]) * (1.0 - lam_init)


def attention_prompt(q, k, v, lam_vecs, subln_g, slopes, lam_init, tq=ATT_TILE):
    b, l, hd = q.shape
    dv = subln_g.shape[0]
    heads = hd // dv
    tq = min(tq, l)
    assert l % tq == 0
    return pl.pallas_call(
        functools.partial(_attn_prompt_kernel, tq=tq, lam_init=lam_init),
        grid_spec=pltpu.PrefetchScalarGridSpec(
            num_scalar_prefetch=1,
            grid=(b, heads, l // tq),
            in_specs=[
                pl.BlockSpec((None, tq, dv), lambda i, h, j, s: (i, j, h)),
                pl.BlockSpec((None, l, dv), lambda i, h, j, s: (i, 0, h)),
                pl.BlockSpec((None, l, dv), lambda i, h, j, s: (i, 0, h)),
                pl.BlockSpec(lam_vecs.shape, lambda i, h, j, s: (0, 0)),
                pl.BlockSpec((1, dv), lambda i, h, j, s: (0, 0)),
            ],
            out_specs=pl.BlockSpec((None, tq, dv), lambda i, h, j, s: (i, j, h)),
        ),
        out_shape=jax.ShapeDtypeStruct((b, l, hd), F32),
        compiler_params=_params("parallel", "parallel", "arbitrary"),
        name="attention_prompt",
    )(slopes, q, k, v, lam_vecs, subln_g.reshape(1, dv))


def _attn_decode_kernel(pt_ref, slope_ref, q_ref, kn_ref, vn_ref, lam_ref, sg_ref, *rest,
                        pp, heads, page, n_new, past_len, lam_init):
    k_refs = rest[:pp]
    v_refs = rest[pp:2 * pp]
    o_ref = rest[2 * pp]
    m_ref, l_ref, acc_ref = rest[2 * pp + 1:]
    j = pl.program_id(1)
    nj = pl.num_programs(1)
    dv = sg_ref.shape[1]
    nr = 2 * SUBLANE
    row = lax.broadcasted_iota(I32, (heads * nr, page), 0)
    qpos = (past_len + (row % SUBLANE)).astype(F32)
    col = lax.broadcasted_iota(I32, (heads * nr, page), 1)
    slope_col = jnp.concatenate([jnp.full((nr, 1), slope_ref[h], F32) for h in range(heads)], axis=0)

    @pl.when(j == 0)
    def _():
        m_ref[...] = jnp.full(m_ref.shape, NEG, F32)
        l_ref[...] = jnp.zeros(l_ref.shape, F32)
        acc_ref[...] = jnp.zeros(acc_ref.shape, F32)

    qv = q_ref[...] * (ATT_HEAD_DIM ** -0.5)
    q2 = [_two_branch_q(qv[:, h * dv:(h + 1) * dv], SUBLANE) for h in range(heads)]

    def update(kb, vb, bias):
        s = jnp.concatenate([_dot_nt(q2[h], kb[:, h * dv:(h + 1) * dv]) for h in range(heads)], axis=0)
        s = s + bias
        m = m_ref[...]
        m_new = jnp.maximum(m, jnp.max(s, axis=-1, keepdims=True))
        alpha = jnp.exp(m - m_new)
        p = jnp.exp(s - m_new)
        l_ref[...] = alpha * l_ref[...] + jnp.sum(p, axis=-1, keepdims=True)
        pb = p.astype(BF16)
        pv = jnp.concatenate([_dot(pb[h * nr:(h + 1) * nr], vb[:, h * dv:(h + 1) * dv])
                              for h in range(heads)], axis=0)
        acc_ref[...] = alpha * acc_ref[...] + pv
        m_ref[...] = m_new

    for p_i in range(pp):
        kpos = ((j * pp + p_i) * page + col).astype(F32)
        update(k_refs[p_i][...].astype(BF16), v_refs[p_i][...].astype(BF16), -slope_col * (qpos - kpos))

    @pl.when(j == nj - 1)
    def _():
        zeros = jnp.zeros((page - SUBLANE, kn_ref.shape[1]), F32)
        kb = jnp.concatenate([kn_ref[...], zeros], axis=0).astype(BF16)
        vb = jnp.concatenate([vn_ref[...], zeros], axis=0).astype(BF16)
        dist = (row % SUBLANE) - col
        ok = (dist >= 0) & (col < n_new)
        update(kb, vb, jnp.where(ok, -slope_col * dist.astype(F32), NEG))
        o = acc_ref[...] / l_ref[...]
        lam = _lambda(lam_ref, lam_init)
        for h in range(heads):
            oh = o[h * nr:h * nr + SUBLANE] - lam * o[h * nr + SUBLANE:(h + 1) * nr]
            o_ref[:, h * dv:(h + 1) * dv] = _rms(oh, sg_ref[...]) * (1.0 - lam_init)


def attention_decode(q, k_new, v_new, cache_k, cache_v, page_table, lam_vecs, subln_g, slopes, lam_init):
    b, n_new, hd = q.shape
    dv = subln_g.shape[0]
    heads = hd // dv
    n_pages = page_table.shape[1]
    page = cache_k.shape[1]
    pp = math.gcd(PAGES_PER_STEP, n_pages)
    assert n_new <= SUBLANE
    padr = ((0, 0), (0, SUBLANE - n_new), (0, 0))
    qp, kp, vp = (jnp.pad(a, padr) for a in (q, k_new, v_new))

    def page_spec(p_i):
        return pl.BlockSpec((None, page, hd), lambda i, j, pt, s: (pt[i * n_pages + j * pp + p_i], 0, 0))

    row_spec = pl.BlockSpec((None, SUBLANE, hd), lambda i, j, pt, s: (i, 0, 0))
    out = pl.pallas_call(
        functools.partial(_attn_decode_kernel, pp=pp, heads=heads, page=page, n_new=n_new,
                          past_len=n_pages * page, lam_init=lam_init),
        grid_spec=pltpu.PrefetchScalarGridSpec(
            num_scalar_prefetch=2,
            grid=(b, n_pages // pp),
            in_specs=[row_spec, row_spec, row_spec,
                      pl.BlockSpec(lam_vecs.shape, lambda i, j, pt, s: (0, 0)),
                      pl.BlockSpec((1, dv), lambda i, j, pt, s: (0, 0))]
                     + [page_spec(p_i) for p_i in range(pp)] * 2,
            out_specs=row_spec,
            scratch_shapes=[pltpu.VMEM((heads * 2 * SUBLANE, 1), F32),
                            pltpu.VMEM((heads * 2 * SUBLANE, 1), F32),
                            pltpu.VMEM((heads * 2 * SUBLANE, dv), F32)],
        ),
        out_shape=jax.ShapeDtypeStruct((b, SUBLANE, hd), F32),
        compiler_params=_params("parallel", "arbitrary"),
        name="attention_decode",
    )(page_table.reshape(-1), slopes, qp, kp, vp, lam_vecs, subln_g.reshape(1, dv),
      *([cache_k] * pp), *([cache_v] * pp))
    return out[:, :n_new]


def _router_kernel(x_ref, g_ref, wr_ref, br_ref, h_ref, idx_ref, gate_ref, cend_ref, run_ref):
    i = pl.program_id(0)
    tm = x_ref.shape[0]

    @pl.when(i == 0)
    def _():
        run_ref[...] = jnp.zeros(run_ref.shape, F32)

    h = _rms(x_ref[...], g_ref[...])
    h_ref[...] = h.astype(BF16)
    logits = lax.dot_general(wr_ref[...], h, (((1,), (1,)), ((), ())), precision=lax.Precision.HIGHEST,
                             preferred_element_type=F32) + br_ref[...]
    ne = logits.shape[0]
    eid = lax.broadcasted_iota(I32, (ne, tm), 0)
    eidf = eid.astype(F32)
    v0 = jnp.max(logits, axis=0, keepdims=True)
    i0 = jnp.min(jnp.where(logits == v0, eidf, float(ne)), axis=0, keepdims=True).astype(I32)
    rest = jnp.where(eid == i0, -jnp.inf, logits)
    v1 = jnp.max(rest, axis=0, keepdims=True)
    i1 = jnp.min(jnp.where(rest == v1, eidf, float(ne)), axis=0, keepdims=True).astype(I32)
    e = jnp.exp(v1 - v0)
    g0 = 1.0 / (1.0 + e)
    g1 = e / (1.0 + e)
    sel0 = eid == i0
    sel1 = eid == i1
    assign = (sel0 | sel1).astype(BF16)
    before = (lax.broadcasted_iota(I32, (tm, tm), 0) < lax.broadcasted_iota(I32, (tm, tm), 1)).astype(BF16)
    rank = _dot(assign, before) + run_ref[...]
    r0 = jnp.sum(jnp.where(sel0, rank, 0.0), axis=0, keepdims=True).astype(I32)
    r1 = jnp.sum(jnp.where(sel1, rank, 0.0), axis=0, keepdims=True).astype(I32)
    idx_ref[...] = jnp.where(eid == 0, i0, jnp.where(eid == 1, i1, jnp.where(eid == 2, r0,
                             jnp.where(eid == 3, r1, 0))))
    gate_ref[...] = jnp.where(eid == 0, g0, jnp.where(eid == 1, g1, 0.0))
    run = run_ref[...] + jnp.sum(assign.astype(F32), axis=1, keepdims=True)
    run_ref[...] = run
    cend_ref[...] = jnp.broadcast_to(run, cend_ref.shape)


def moe_route(x, g, w_router, b_router, tc=MOE_CHUNK):
    t, d = x.shape
    ne = w_router.shape[1]
    assert t % tc == 0 and ne == SUBLANE
    nchunk = t // tc
    return pl.pallas_call(
        _router_kernel,
        grid=(nchunk,),
        in_specs=[pl.BlockSpec((tc, d), lambda i: (i, 0)), _resident((1, d)),
                  _resident((ne, d)), _resident((ne, 1))],
        out_specs=[pl.BlockSpec((tc, d), lambda i: (i, 0)),
                   pl.BlockSpec((ne, tc), lambda i: (0, i)),
                   pl.BlockSpec((ne, tc), lambda i: (0, i)),
                   pl.BlockSpec((None, ne, LANE), lambda i: (i, 0, 0))],
        out_shape=[jax.ShapeDtypeStruct((t, d), BF16), jax.ShapeDtypeStruct((ne, t), I32),
                   jax.ShapeDtypeStruct((ne, t), F32), jax.ShapeDtypeStruct((nchunk, ne, LANE), F32)],
        scratch_shapes=[pltpu.VMEM((ne, 1), F32)],
        compiler_params=_params("arbitrary"),
        name="moe_route",
    )(x, g.reshape(1, d), w_router.T, b_router.reshape(ne, 1))


def _slot_rows(idx_ref, off_ref, slot0, ne):
    idx = idx_ref[...]
    e0, e1, r0, r1 = idx[0:1], idx[1:2], idx[2:3], idx[3:4]
    o0 = jnp.zeros_like(e0)
    o1 = jnp.zeros_like(e1)
    for k in range(ne):
        o0 = jnp.where(e0 == k, off_ref[k], o0)
        o1 = jnp.where(e1 == k, off_ref[k], o1)
    return o0 + r0 - slot0, o1 + r1 - slot0


def _moe_ffn_kernel(wt_ref, wc_ref, wf_ref, te_ref, off_ref, idx_ref, h_ref, wg_ref, wu_ref, wd_ref,
                    o_ref, acc_ref, *, ts, ne):
    w = pl.program_id(0)
    flags = wf_ref[w]

    @pl.when((flags & 2) != 0)
    def _():
        acc_ref[...] = jnp.zeros(acc_ref.shape, F32)

    @pl.when((flags & 1) != 0)
    def _():
        sa, sb = _slot_rows(idx_ref, off_ref, wt_ref[w] * ts, ne)
        rows = lax.broadcasted_iota(I32, (ts, idx_ref.shape[1]), 0)
        pick = ((rows == sa) | (rows == sb)).astype(BF16)
        acc_ref[...] += _dot(pick, h_ref[...])

    @pl.when((flags & 4) != 0)
    def _():
        xb = acc_ref[...].astype(BF16)
        a = _dot(xb, wg_ref[...])
        u = _dot(xb, wu_ref[...])
        o_ref[...] = _dot((_silu(a) * u).astype(BF16), wd_ref[...]).astype(BF16)


def _moe_combine_kernel(wc_ref, wt_ref, wf_ref, off_ref, idx_ref, gate_ref, x_ref, es_ref, o_ref, acc_ref,
                        *, ts, ne):
    w = pl.program_id(0)
    flags = wf_ref[w]

    @pl.when((flags & 2) != 0)
    def _():
        acc_ref[...] = x_ref[...]

    @pl.when((flags & 1) != 0)
    def _():
        sa, sb = _slot_rows(idx_ref, off_ref, wt_ref[w] * ts, ne)
        gt = gate_ref[...]
        rows = lax.broadcasted_iota(I32, (ts, idx_ref.shape[1]), 0)
        wgt = (jnp.where(rows == sa, gt[0:1], 0.0) + jnp.where(rows == sb, gt[1:2], 0.0)).astype(BF16)
        acc_ref[...] += _dot_tn(wgt, es_ref[...])

    @pl.when((flags & 4) != 0)
    def _():
        o_ref[...] = acc_ref[...]


def _work_lists(cend, t, tc, ts, ne):
    nchunk = t // tc
    nt_max = (2 * t) // ts + ne
    w_max = nt_max + ne * nchunk
    cend = cend.astype(I32)
    cstart = jnp.concatenate([jnp.zeros((1, ne), I32), cend[:-1]], axis=0)
    cnt = cend[-1]
    tiles = (cnt + ts - 1) // ts
    tile_end = jnp.cumsum(tiles)
    off = (tile_end - tiles) * ts
    n_tiles = tile_end[-1]
    tile_ids = jnp.arange(nt_max, dtype=I32)
    te = jnp.minimum(jnp.searchsorted(tile_end, tile_ids, side="right"), ne - 1).astype(I32)

    def flatten(counts, w_total):
        ends = jnp.cumsum(counts)
        total = ends[-1]
        wi = jnp.minimum(jnp.arange(w_total, dtype=I32), total - 1)
        owner = jnp.searchsorted(ends, wi, side="right").astype(I32)
        local = wi - (ends[owner] - counts[owner])
        valid = jnp.arange(w_total, dtype=I32) < total
        return owner, local, valid

    k0 = tile_ids * ts - off[te]
    k1 = jnp.minimum(k0 + ts, cnt[te])
    ce_t = cend[:, te]
    cs_t = cstart[:, te]
    c_lo = jnp.sum(ce_t <= k0[None, :], axis=0).astype(I32)
    c_hi = (nchunk - 1 - jnp.sum(cs_t >= k1[None, :], axis=0)).astype(I32)
    active = tile_ids < n_tiles
    n_items = jnp.where(active, c_hi - c_lo + 1, 0)
    owner, local, valid = flatten(n_items, w_max)
    f_tile, f_chunk = owner, c_lo[owner] + local
    first = local == 0
    last = local == n_items[owner] - 1
    f_flags = (valid * (1 + 2 * first + 4 * last)).astype(I32)

    has = cend > cstart
    s_lo = (off[None, :] + cstart) // ts
    s_hi = (off[None, :] + cend - 1) // ts
    n_ce = jnp.where(has, s_hi - s_lo + 1, 0).reshape(-1)
    owner, local, valid = flatten(n_ce, w_max)
    c_chunk = owner // ne
    c_tile = s_lo.reshape(-1)[owner] + local
    per_chunk = jnp.sum(n_ce.reshape(nchunk, ne), axis=1)
    chunk_end = jnp.cumsum(per_chunk)
    wi = jnp.minimum(jnp.arange(w_max, dtype=I32), chunk_end[-1] - 1)
    first = wi == (chunk_end - per_chunk)[c_chunk]
    last = wi == chunk_end[c_chunk] - 1
    c_flags = (valid * (1 + 2 * first + 4 * last)).astype(I32)
    return (f_tile.astype(I32), f_chunk.astype(I32), f_flags, te, off.astype(I32),
            c_chunk.astype(I32), c_tile.astype(I32), c_flags, nt_max, w_max)


def moe(x, g, w_router, b_router, wg, wu, wd, tc=MOE_CHUNK, ts=MOE_SLOT_TILE):
    t, d = x.shape
    ne, _, f = wg.shape
    tc = min(tc, t)
    h, idx, gate, cend = moe_route(x, g, w_router, b_router, tc)
    (f_tile, f_chunk, f_flags, te, off, c_chunk, c_tile, c_flags, nt_max, w_max) = _work_lists(
        cend[:, :, 0], t, tc, ts, ne)

    sorted_out = pl.pallas_call(
        functools.partial(_moe_ffn_kernel, ts=ts, ne=ne),
        grid_spec=pltpu.PrefetchScalarGridSpec(
            num_scalar_prefetch=5,
            grid=(w_max,),
            in_specs=[
                pl.BlockSpec((ne, tc), lambda w, wt, wc, wf, te_, of: (0, wc[w])),
                pl.BlockSpec((tc, d), lambda w, wt, wc, wf, te_, of: (wc[w], 0)),
                pl.BlockSpec((None, d, f), lambda w, wt, wc, wf, te_, of: (te_[wt[w]], 0, 0)),
                pl.BlockSpec((None, d, f), lambda w, wt, wc, wf, te_, of: (te_[wt[w]], 0, 0)),
                pl.BlockSpec((None, f, d), lambda w, wt, wc, wf, te_, of: (te_[wt[w]], 0, 0)),
            ],
            out_specs=pl.BlockSpec((ts, d), lambda w, wt, wc, wf, te_, of: (wt[w], 0)),
            scratch_shapes=[pltpu.VMEM((ts, d), F32)],
        ),
        out_shape=jax.ShapeDtypeStruct((nt_max * ts, d), BF16),
        compiler_params=_params("arbitrary"),
        name="moe_ffn",
    )(f_tile, f_chunk, f_flags, te, off, idx, h, wg, wu, wd)

    return pl.pallas_call(
        functools.partial(_moe_combine_kernel, ts=ts, ne=ne),
        grid_spec=pltpu.PrefetchScalarGridSpec(
            num_scalar_prefetch=4,
            grid=(w_max,),
            in_specs=[
                pl.BlockSpec((ne, tc), lambda w, wc, wt, wf, of: (0, wc[w])),
                pl.BlockSpec((ne, tc), lambda w, wc, wt, wf, of: (0, wc[w])),
                pl.BlockSpec((tc, d), lambda w, wc, wt, wf, of: (wc[w], 0)),
                pl.BlockSpec((ts, d), lambda w, wc, wt, wf, of: (wt[w], 0)),
            ],
            out_specs=pl.BlockSpec((tc, d), lambda w, wc, wt, wf, of: (wc[w], 0)),
            scratch_shapes=[pltpu.VMEM((tc, d), F32)],
        ),
        out_shape=jax.ShapeDtypeStruct((t, d), F32),
        compiler_params=_params("arbitrary"),
        name="moe_combine",
    )(c_chunk, c_tile, c_flags, off, idx, gate, x, sorted_out)


def _lambda_init(layer):
    return 0.8 - 0.6 * math.exp(-0.3 * layer)


def kernel(x_prompt, x_sample, state_ssm, state_conv, cache_k, cache_v, page_table, norm_mix_g, norm_ffn_g, norm_kv_g, norm_final_g, ssm_w_in, ssm_conv_w, ssm_conv_b, ssm_dt_bias, ssm_a_log, ssm_d, ssm_norm_g, ssm_w_out, kv_w_k, kv_w_v, att_w_q, att_lam_q1, att_lam_k1, att_lam_q2, att_lam_k2, att_subln_g, att_w_o, ffn_w_gate, ffn_w_up, ffn_w_down, moe_w_router, moe_b_router, moe_w_gate, moe_w_up, moe_w_down):
    depth, d_model = norm_mix_g.shape
    n_a = ssm_w_in.shape[0]
    heads_ssm = ssm_a_log.shape[1]
    d_inner = heads_ssm * SSM_HEAD_DIM
    conv_dim = ssm_conv_w.shape[2]
    att_dv = att_subln_g.shape[1]
    att_heads = kv_w_v.shape[1] // att_dv
    slopes = 2.0 ** (-8.0 * jnp.arange(1, att_heads + 1, dtype=F32) / att_heads)

    in_dim = ssm_w_in.shape[2]
    w_in = jnp.pad(ssm_w_in, ((0, 0), (0, 0), (0, d_inner + conv_dim + LANE - in_dim))).astype(BF16)
    w_out = ssm_w_out.astype(BF16)
    w_kv = jnp.concatenate([kv_w_k, kv_w_v], axis=1).astype(BF16)
    w_q = att_w_q.astype(BF16)
    w_o = att_w_o.astype(BF16)
    f_gate, f_up, f_down = ffn_w_gate.astype(BF16), ffn_w_up.astype(BF16), ffn_w_down.astype(BF16)
    m_gate, m_up, m_down = moe_w_gate.astype(BF16), moe_w_up.astype(BF16), moe_w_down.astype(BF16)
    lam_vecs = jnp.stack([att_lam_q1, att_lam_k1, att_lam_q2, att_lam_k2], axis=1)

    def trunk(x3, ssm0, conv0, past):
        b, l, _ = x3.shape
        x = x3.reshape(b * l, d_model)
        ssm_new, conv_new = [], []
        k_new = v_new = None
        for layer in range(depth):
            if layer < n_a:
                zx = linear(x, w_in[layer], g=norm_mix_g[layer]).reshape(b, l, -1)
                valid = SSD_CHUNK if l % SSD_CHUNK == 0 else l
                y, s, cv = ssd_mixer(zx, conv0[layer], ssm0[layer].reshape(b, d_inner, SSM_STATE),
                                     ssm_conv_w[layer], ssm_conv_b[layer], ssm_dt_bias[layer],
                                     ssm_a_log[layer], ssm_d[layer], ssm_norm_g[layer], valid)
                ssm_new.append(s.reshape(b, heads_ssm, SSM_HEAD_DIM, SSM_STATE))
                conv_new.append(cv)
                x = linear(y.reshape(b * l, d_inner), w_out[layer], residual=x)
            else:
                j = layer - n_a
                q = linear(x, w_q[j], g=norm_mix_g[layer]).reshape(b, l, -1)
                lam_init = _lambda_init(layer)
                if past is None:
                    o = attention_prompt(q, k_new, v_new, lam_vecs[j], att_subln_g[j], slopes, lam_init)
                else:
                    o = attention_decode(q, k_new, v_new, past[0], past[1], page_table, lam_vecs[j],
                                         att_subln_g[j], slopes, lam_init)
                x = linear(o.reshape(b * l, -1), w_o[j], residual=x)
            i = layer // 2
            if layer % 2 == 0:
                x = ffn(x, norm_ffn_g[layer], f_gate[i], f_up[i], f_down[i])
            else:
                x = moe(x, norm_ffn_g[layer], moe_w_router[i], moe_b_router[i], m_gate[i], m_up[i], m_down[i])
            if layer == n_a - 1:
                kv = linear(x, w_kv, g=norm_kv_g)
                k_new = kv[:, :att_heads * att_dv].reshape(b, l, att_heads * att_dv)
                v_new = kv[:, att_heads * att_dv:].reshape(b, l, att_heads * att_dv)
        y = rmsnorm(x, norm_final_g).reshape(b, l, d_model)
        return (y, jnp.stack(ssm_new), jnp.stack(conv_new),
                k_new.reshape(b, l, att_heads, att_dv), v_new.reshape(b, l, att_heads, att_dv))

    bp = x_prompt.shape[0]
    ssm0_p = jnp.zeros((n_a, bp, heads_ssm, SSM_HEAD_DIM, SSM_STATE), F32)
    conv0_p = jnp.zeros((n_a, bp, CONV_WIDTH - 1, conv_dim), F32)
    y_p, ssm_p, conv_p, k_p, v_p = trunk(x_prompt, ssm0_p, conv0_p, None)

    n_pool, page = cache_k.shape[:2]
    past = (cache_k.reshape(n_pool, page, -1), cache_v.reshape(n_pool, page, -1))
    y_s, ssm_s, conv_s, k_s, v_s = trunk(x_sample, state_ssm, state_conv, past)
    return (y_p, y_s, ssm_p, conv_p, k_p, v_p, ssm_s, conv_s, k_s, v_s)
```

```python
import functools
import math

import jax
import jax.numpy as jnp
from jax import lax
from jax.experimental import pallas as pl
from jax.experimental.pallas import tpu as pltpu

F32 = jnp.float32
BF16 = jnp.bfloat16
I32 = jnp.int32

RMS_EPS = 1e-5
SSM_HEAD_DIM = 64
SSM_GROUPS = 8
SSM_STATE = 128
CONV_WIDTH = 4
SSD_CHUNK = 128
ATT_HEAD_DIM = 64
N_EXPERTS = 8
LANE = 128
SUBLANE = 8
VMEM_LIMIT = 56 * 1024 * 1024
NEG = -1e30

TOKEN_TILE = 256
MOE_SLOT_TILE = 256
MOE_CHUNK = 1024
ATT_TILE = 256
PAGES_PER_STEP = 8


def _params(*sem):
    return pltpu.CompilerParams(dimension_semantics=sem, vmem_limit_bytes=VMEM_LIMIT)


def _resident(shape):
    nd = len(shape)
    return pl.BlockSpec(shape, lambda *_: (0,) * nd, pipeline_mode=pl.Buffered(1))


def _rms(x, g):
    ms = jnp.mean(x * x, axis=-1, keepdims=True)
    return x * lax.rsqrt(ms + RMS_EPS) * g


def _silu(x):
    return x * jax.nn.sigmoid(x)


def _dot(a, b):
    return jnp.dot(a, b, preferred_element_type=F32)


def _dot_nt(a, b):
    return lax.dot_general(a, b, (((1,), (1,)), ((), ())), preferred_element_type=F32)


def _dot_tn(a, b):
    return lax.dot_general(a, b, (((0,), (0,)), ((), ())), preferred_element_type=F32)


def _split2(x):
    hi = x.astype(BF16)
    lo = (x - hi.astype(F32)).astype(BF16)
    return hi, lo


def _split3(x):
    hi = x.astype(BF16)
    r = x - hi.astype(F32)
    mid = r.astype(BF16)
    lo = (r - mid.astype(F32)).astype(BF16)
    return hi, mid, lo


def _linear_kernel(*refs, norm, residual, scale, outs):
    it = iter(refs)
    x_ref = next(it)
    g_ref = next(it) if norm else None
    w_ref = next(it)
    r_ref = next(it) if residual else None
    x = x_ref[...].astype(F32)
    if norm:
        x = _rms(x, g_ref[...])
    y = _dot(x.astype(BF16), w_ref[...])
    if residual:
        y = y + r_ref[...]
    if scale is not None:
        y = y * scale
    for kind in outs:
        o_ref = next(it)
        if kind == "f32":
            o_ref[...] = y
        elif kind == "bf16":
            o_ref[...] = y.astype(BF16)
        else:
            o_ref[...] = jnp.transpose(y).astype(BF16)


def linear(x, w, g=None, residual=None, scale=None, outs=("f32",), tm=TOKEN_TILE):
    t, k = x.shape
    n = w.shape[1]
    tm = min(tm, t)
    assert t % tm == 0
    args = [x]
    specs = [pl.BlockSpec((tm, k), lambda i: (i, 0))]
    if g is not None:
        args.append(g.reshape(1, k))
        specs.append(_resident((1, k)))
    args.append(w)
    specs.append(_resident((k, n)))
    if residual is not None:
        args.append(residual)
        specs.append(pl.BlockSpec((tm, n), lambda i: (i, 0)))
    out_specs, out_shapes = [], []
    for kind in outs:
        if kind == "bf16_t":
            out_specs.append(pl.BlockSpec((None, n, tm), lambda i: (i, 0, 0)))
            out_shapes.append(jax.ShapeDtypeStruct((t // tm, n, tm), BF16))
        else:
            out_specs.append(pl.BlockSpec((tm, n), lambda i: (i, 0)))
            out_shapes.append(jax.ShapeDtypeStruct((t, n), F32 if kind == "f32" else BF16))
    res = pl.pallas_call(
        functools.partial(_linear_kernel, norm=g is not None, residual=residual is not None,
                          scale=scale, outs=tuple(outs)),
        grid=(t // tm,),
        in_specs=specs,
        out_specs=out_specs,
        out_shape=out_shapes,
        compiler_params=_params("parallel"),
        name="linear",
    )(*args)
    return res[0] if len(outs) == 1 else res


def _ffn_kernel(x_ref, g_ref, wg_ref, wu_ref, wd_ref, o_ref):
    x = x_ref[...]
    h = _rms(x, g_ref[...]).astype(BF16)
    a = _dot(h, wg_ref[...])
    u = _dot(h, wu_ref[...])
    act = (_silu(a) * u).astype(BF16)
    o_ref[...] = x + _dot(act, wd_ref[...])


def ffn(x, g, wg, wu, wd, tm=TOKEN_TILE):
    t, d = x.shape
    f = wg.shape[1]
    tm = min(tm, t)
    assert t % tm == 0
    return pl.pallas_call(
        _ffn_kernel,
        grid=(t // tm,),
        in_specs=[pl.BlockSpec((tm, d), lambda i: (i, 0)), _resident((1, d)),
                  _resident((d, f)), _resident((d, f)), _resident((f, d))],
        out_specs=pl.BlockSpec((tm, d), lambda i: (i, 0)),
        out_shape=jax.ShapeDtypeStruct((t, d), F32),
        compiler_params=_params("parallel"),
        name="ffn",
    )(x, g.reshape(1, d), wg, wu, wd)


def _rmsnorm_kernel(x_ref, g_ref, o_ref):
    o_ref[...] = _rms(x_ref[...], g_ref[...])


def rmsnorm(x, g, tm=TOKEN_TILE):
    t, d = x.shape
    tm = min(tm, t)
    assert t % tm == 0
    return pl.pallas_call(
        _rmsnorm_kernel,
        grid=(t // tm,),
        in_specs=[pl.BlockSpec((tm, d), lambda i: (i, 0)), _resident((1, d))],
        out_specs=pl.BlockSpec((tm, d), lambda i: (i, 0)),
        out_shape=jax.ShapeDtypeStruct((t, d), F32),
        compiler_params=_params("parallel"),
        name="rmsnorm",
    )(x, g.reshape(1, d))


def _ssd_kernel(zx_ref, conv0_ref, ssm0_ref, cw_ref, cb_ref, dtb_ref, alog_ref, dsk_ref,
                ng_ref, exp_ref, y_ref, sfin_ref, cnew_ref, xp_ref, stage_ref,
                *, q, rows, valid, d_inner, heads):
    c = pl.program_id(1)
    nc = pl.num_programs(1)
    kp = SSD_CHUNK
    gn = SSM_GROUPS * SSM_STATE
    hpg = heads // SSM_GROUPS
    gw = hpg * SSM_HEAD_DIM

    @pl.when(c == 0)
    def _():
        sfin_ref[...] = ssm0_ref[...]
        xp_ref[0:SUBLANE, :] = jnp.zeros((SUBLANE, xp_ref.shape[1]), F32)
        xp_ref[SUBLANE - (CONV_WIDTH - 1):SUBLANE, :] = conv0_ref[...]

    if rows < q:
        stage_ref[...] = jnp.zeros(stage_ref.shape, F32)
        stage_ref[0:rows, :] = zx_ref[...]
        zx = stage_ref[...]
    else:
        zx = zx_ref[...]
    z = zx[:, :d_inner]
    xbc = zx[:, d_inner:2 * d_inner + 2 * gn]
    dtr = zx[:, 2 * d_inner + 2 * gn:]

    xp_ref[SUBLANE:SUBLANE + q, :] = xbc
    base = SUBLANE - (CONV_WIDTH - 1)
    conv = cb_ref[...] + xp_ref[base:base + q, :] * cw_ref[0:1, :]
    for k in range(1, CONV_WIDTH):
        conv = conv + xp_ref[base + k:base + k + q, :] * cw_ref[k:k + 1, :]
    tail = xp_ref[SUBLANE + valid - (CONV_WIDTH - 1):SUBLANE + valid, :]
    xp_ref[base:SUBLANE, :] = tail

    @pl.when(c == nc - 1)
    def _():
        cnew_ref[...] = tail

    xa = _silu(conv)
    xs = xa[:, :d_inner]
    bm = xa[:, d_inner:d_inner + gn]
    cm = xa[:, d_inner + gn:]

    def pad(v):
        if q == kp:
            return v
        return jnp.concatenate([v, jnp.zeros((kp - q, v.shape[1]), v.dtype)], axis=0)

    row = lax.broadcasted_iota(I32, (q, kp), 0)
    col = lax.broadcasted_iota(I32, (q, kp), 1)
    causal = col <= row

    dtv = dtr + dtb_ref[...]
    dt = jnp.maximum(dtv, 0.0) + jnp.log1p(jnp.exp(-jnp.abs(dtv)))
    if valid < q:
        dt = jnp.where(lax.broadcasted_iota(I32, (q, LANE), 0) < valid, dt, 0.0)
    adt = dt * (-jnp.exp(alog_ref[...]))
    tril = causal.astype(BF16)
    cs = sum(_dot(tril, part) for part in _split3(pad(adt)))
    cs_t = jnp.transpose(pad(cs))
    cs_last = cs[q - 1:q, :]
    e_last = jnp.exp(cs_last)

    stacked = jnp.concatenate([dt, jnp.exp(cs), jnp.exp(cs_last - cs)], axis=0)
    hi, lo = _split2(stacked)
    wide = _dot(hi, exp_ref[...]) + _dot(lo, exp_ref[...])
    dt_w, ecs_w, dout_w = wide[0:q], wide[q:2 * q], wide[2 * q:3 * q]
    xdt = xs * dt_w
    xdtd = pad(xdt * dout_w).astype(BF16)
    xdt_p = pad(xdt)
    bm_p = pad(bm).astype(BF16)
    cm_b = cm.astype(BF16)
    lane_g = lax.broadcasted_iota(I32, (kp, gw), 1) // SSM_HEAD_DIM

    for g in range(SSM_GROUPS):
        b_g = bm_p[:, g * SSM_STATE:(g + 1) * SSM_STATE]
        c_g = cm_b[:, g * SSM_STATE:(g + 1) * SSM_STATE]
        cb = _dot_nt(c_g, b_g)
        xg = xdt_p[:, g * gw:(g + 1) * gw]
        m_parts, x_parts, scale_parts = [], [], []
        for r in range(hpg):
            h = g * hpg + r
            seg = cs[:, h:h + 1] - cs_t[h:h + 1, :]
            decay = jnp.exp(jnp.where(causal, seg, NEG))
            m_parts.append((cb * decay).astype(BF16))
            x_parts.append(jnp.where(lane_g == r, xg, 0.0).astype(BF16))
            scale_parts.append(jnp.broadcast_to(e_last[:, h:h + 1], (SSM_HEAD_DIM, SSM_STATE)))
        y_diag = _dot(jnp.concatenate(m_parts, axis=1), jnp.concatenate(x_parts, axis=0))
        s_g = sfin_ref[g * gw:(g + 1) * gw, :]
        y_off = _dot_nt(c_g, s_g.astype(BF16)) * ecs_w[:, g * gw:(g + 1) * gw]
        new = _dot_tn(xdtd[:, g * gw:(g + 1) * gw], b_g)
        sfin_ref[g * gw:(g + 1) * gw, :] = s_g * jnp.concatenate(scale_parts, axis=0) + new
        y = y_diag + y_off + dsk_ref[:, g * gw:(g + 1) * gw] * xs[:, g * gw:(g + 1) * gw]
        y = y * _silu(z[:, g * gw:(g + 1) * gw])
        y = y * lax.rsqrt(jnp.mean(y * y, axis=-1, keepdims=True) + RMS_EPS)
        y_ref[:, g * gw:(g + 1) * gw] = y * ng_ref[:, g * gw:(g + 1) * gw]


def ssd_mixer(zx, conv0, ssm0, conv_w, conv_b, dt_bias, a_log, d_skip, norm_g, valid):
    b, l, width = zx.shape
    heads = a_log.shape[0]
    d_inner = heads * SSM_HEAD_DIM
    conv_dim = conv_w.shape[1]
    assert width == d_inner + conv_dim + LANE
    if l % SSD_CHUNK == 0:
        q = rows = SSD_CHUNK
        nc = l // SSD_CHUNK
        assert valid == SSD_CHUNK
    else:
        assert l < SSD_CHUNK
        rows, nc = l, 1
        q = -(-l // SUBLANE) * SUBLANE
    padh = LANE - heads
    expand = jnp.repeat(jnp.eye(LANE, heads, dtype=BF16), SSM_HEAD_DIM, axis=1)
    kern = functools.partial(_ssd_kernel, q=q, rows=rows, valid=valid, d_inner=d_inner, heads=heads)
    y, sfin, cnew = pl.pallas_call(
        kern,
        grid=(b, nc),
        in_specs=[
            pl.BlockSpec((None, rows, width), lambda i, j: (i, j, 0)),
            pl.BlockSpec((None, CONV_WIDTH - 1, conv_dim), lambda i, j: (i, 0, 0)),
            pl.BlockSpec((None, d_inner, SSM_STATE), lambda i, j: (i, 0, 0)),
            _resident((CONV_WIDTH, conv_dim)), _resident((1, conv_dim)),
            _resident((1, LANE)), _resident((1, LANE)),
            _resident((1, d_inner)), _resident((1, d_inner)), _resident((LANE, d_inner)),
        ],
        out_specs=[
            pl.BlockSpec((None, q, d_inner), lambda i, j: (i, j, 0)),
            pl.BlockSpec((None, d_inner, SSM_STATE), lambda i, j: (i, 0, 0)),
            pl.BlockSpec((None, CONV_WIDTH - 1, conv_dim), lambda i, j: (i, 0, 0)),
        ],
        out_shape=[
            jax.ShapeDtypeStruct((b, nc * q, d_inner), F32),
            jax.ShapeDtypeStruct((b, d_inner, SSM_STATE), F32),
            jax.ShapeDtypeStruct((b, CONV_WIDTH - 1, conv_dim), F32),
        ],
        scratch_shapes=[pltpu.VMEM((SUBLANE + q, conv_dim), F32),
                        pltpu.VMEM((q, width) if rows < q else (SUBLANE, LANE), F32)],
        compiler_params=_params("parallel", "arbitrary"),
        name="ssd_mixer",
    )(zx, conv0, ssm0, conv_w, conv_b.reshape(1, conv_dim),
      jnp.pad(dt_bias, (0, padh)).reshape(1, LANE), jnp.pad(a_log, (0, padh)).reshape(1, LANE),
      jnp.repeat(d_skip, SSM_HEAD_DIM).reshape(1, d_inner), norm_g.reshape(1, d_inner), expand)
    return y[:, :l], sfin, cnew


def _lambda(lam_ref, lam_init):
    lv = lam_ref[...]
    s1 = jnp.sum(lv[0:1] * lv[1:2], axis=-1, keepdims=True)
    s2 = jnp.sum(lv[2:3] * lv[3:4], axis=-1, keepdims=True)
    return jnp.exp(s1) - jnp.exp(s2) + lam_init


def _two_branch_q(qv, n):
    lane = lax.broadcasted_iota(I32, qv.shape, 1)
    zero = jnp.zeros_like(qv)
    return jnp.concatenate([jnp.where(lane < ATT_HEAD_DIM, qv, zero),
                            jnp.where(lane >= ATT_HEAD_DIM, qv, zero)], axis=0).astype(BF16)


def _attn_prompt_kernel(slope_ref, qt_ref, k_ref, vt_ref, lam_ref, sg_ref, o_ref, *, tq, lam_init):
    h = pl.program_id(1)
    qi = pl.program_id(2)
    slope = slope_ref[h]
    dv = qt_ref.shape[0]
    qt = qt_ref[...]
    feat = lax.broadcasted_iota(I32, qt.shape, 0)
    zero = jnp.zeros_like(qt)
    q2t = jnp.concatenate([jnp.where(feat < ATT_HEAD_DIM, qt, zero),
                           jnp.where(feat >= ATT_HEAD_DIM, qt, zero)], axis=1)
    cc = lax.broadcasted_iota(I32, (tq, 2 * tq), 1)
    rel = jnp.where(cc >= tq, cc - tq, cc) - lax.broadcasted_iota(I32, (tq, 2 * tq), 0)
    bias = -slope * rel.astype(F32)

    def scores(j):
        start = pl.multiple_of(j * tq, tq)
        return _dot(k_ref[pl.ds(start, tq), :], q2t) + bias

    def consume(j, s, carry, diagonal):
        m, l, acc = carry
        if diagonal:
            s = jnp.where(rel >= 0, s, NEG)
        tile_bias = -slope * ((qi - j) * tq).astype(F32)
        m_new = jnp.maximum(m, jnp.max(s, axis=0, keepdims=True) + tile_bias)
        alpha = jnp.exp(m - m_new)
        p = jnp.exp(s - (m_new - tile_bias))
        l = alpha * l + jnp.sum(p, axis=0, keepdims=True)
        acc = alpha * acc + _dot(vt_ref[j], p.astype(BF16))
        return m_new, l, acc

    m0 = jnp.full((1, 2 * tq), NEG, F32)
    l0 = jnp.zeros((1, 2 * tq), F32)
    a0 = jnp.zeros((dv, 2 * tq), F32)

    def body(j, c):
        s_next = scores(j + 1)
        return consume(j, c[3], c[:3], False) + (s_next,)

    c = lax.fori_loop(0, qi, body, (m0, l0, a0, scores(0)))
    _, l, acc = consume(qi, c[3], c[:3], True)
    o = acc / l
    o = jnp.transpose(o[:, :tq] - _lambda(lam_ref, lam_init) * o[:, tq:])
    o_ref[...] = _rms(o, sg_ref[...]) * (1.0 - lam_init)


def attention_prompt(qt, k, vt, lam_vecs, subln_g, slopes, lam_init):
    b, nq, hd, tq = qt.shape
    l = nq * tq
    dv = subln_g.shape[0]
    heads = hd // dv
    return pl.pallas_call(
        functools.partial(_attn_prompt_kernel, tq=tq, lam_init=lam_init),
        grid_spec=pltpu.PrefetchScalarGridSpec(
            num_scalar_prefetch=1,
            grid=(b, heads, nq),
            in_specs=[
                pl.BlockSpec((None, None, dv, tq), lambda i, h, j, s: (i, j, h, 0)),
                pl.BlockSpec((None, l, dv), lambda i, h, j, s: (i, 0, h)),
                pl.BlockSpec((None, nq, dv, tq), lambda i, h, j, s: (i, 0, h, 0)),
                pl.BlockSpec(lam_vecs.shape, lambda i, h, j, s: (0, 0)),
                pl.BlockSpec((1, dv), lambda i, h, j, s: (0, 0)),
            ],
            out_specs=pl.BlockSpec((None, tq, dv), lambda i, h, j, s: (i, j, h)),
        ),
        out_shape=jax.ShapeDtypeStruct((b, l, hd), F32),
        compiler_params=_params("parallel", "parallel", "arbitrary"),
        name="attention_prompt",
    )(slopes, qt, k, vt, lam_vecs, subln_g.reshape(1, dv))


def _attn_decode_kernel(pt_ref, slope_ref, q_ref, kn_ref, vn_ref, lam_ref, sg_ref, *rest,
                        pp, heads, page, n_new, past_len, lam_init):
    k_refs = rest[:pp]
    v_refs = rest[pp:2 * pp]
    o_ref = rest[2 * pp]
    m_ref, l_ref, acc_ref = rest[2 * pp + 1:]
    j = pl.program_id(1)
    nj = pl.num_programs(1)
    dv = sg_ref.shape[1]
    nr = 2 * SUBLANE
    row = lax.broadcasted_iota(I32, (heads * nr, page), 0)
    col = lax.broadcasted_iota(I32, (heads * nr, page), 1)
    slope_col = jnp.concatenate([jnp.full((nr, 1), slope_ref[h], F32) for h in range(heads)], axis=0)

    @pl.when(j == 0)
    def _():
        m_ref[...] = jnp.full(m_ref.shape, NEG, F32)
        l_ref[...] = jnp.zeros(l_ref.shape, F32)
        acc_ref[...] = jnp.zeros(acc_ref.shape, F32)

    qv = q_ref[...] * (ATT_HEAD_DIM ** -0.5)
    q2 = [_two_branch_q(qv[:, h * dv:(h + 1) * dv], SUBLANE) for h in range(heads)]

    def update(k_of, v_of, bias, n_pg):
        s = jnp.concatenate([jnp.concatenate([_dot_nt(q2[h], k_of(p, h)) for h in range(heads)], axis=0)
                             for p in range(n_pg)], axis=1) + bias
        m = m_ref[...]
        m_new = jnp.maximum(m, jnp.max(s, axis=-1, keepdims=True))
        alpha = jnp.exp(m - m_new)
        pr = jnp.exp(s - m_new)
        l_ref[...] = alpha * l_ref[...] + jnp.sum(pr, axis=-1, keepdims=True)
        pb = pr.astype(BF16)
        pv = jnp.concatenate(
            [sum(_dot(pb[h * nr:(h + 1) * nr, p * page:(p + 1) * page], v_of(p, h)) for p in range(n_pg))
             for h in range(heads)], axis=0)
        acc_ref[...] = alpha * acc_ref[...] + pv
        m_ref[...] = m_new

    rows_w = lax.broadcasted_iota(I32, (heads * nr, pp * page), 0)
    kpos = (j * (pp * page) + lax.broadcasted_iota(I32, (heads * nr, pp * page), 1)).astype(F32)
    qpos_w = (past_len + (rows_w % SUBLANE)).astype(F32)
    update(lambda p, h: k_refs[p][pl.ds(h, page, stride=heads), :].astype(BF16),
           lambda p, h: v_refs[p][pl.ds(h, page, stride=heads), :].astype(BF16),
           -slope_col * (qpos_w - kpos), pp)

    @pl.when(j == nj - 1)
    def _():
        zeros = jnp.zeros((page - SUBLANE, kn_ref.shape[1]), F32)
        kb = jnp.concatenate([kn_ref[...], zeros], axis=0).astype(BF16)
        vb = jnp.concatenate([vn_ref[...], zeros], axis=0).astype(BF16)
        dist = (row % SUBLANE) - col
        ok = (dist >= 0) & (col < n_new)
        update(lambda p, h: kb[:, h * dv:(h + 1) * dv], lambda p, h: vb[:, h * dv:(h + 1) * dv],
               jnp.where(ok, -slope_col * dist.astype(F32), NEG), 1)
        o = acc_ref[...] / l_ref[...]
        lam = _lambda(lam_ref, lam_init)
        for h in range(heads):
            oh = o[h * nr:h * nr + SUBLANE] - lam * o[h * nr + SUBLANE:(h + 1) * nr]
            o_ref[:, h * dv:(h + 1) * dv] = _rms(oh, sg_ref[...]) * (1.0 - lam_init)


def attention_decode(q, k_new, v_new, cache_k, cache_v, page_table, lam_vecs, subln_g, slopes, lam_init):
    b, n_new, hd = q.shape
    dv = subln_g.shape[0]
    heads = hd // dv
    n_pages = page_table.shape[1]
    page = cache_k.shape[1]
    pp = math.gcd(PAGES_PER_STEP, n_pages)
    assert n_new <= SUBLANE
    padr = ((0, 0), (0, SUBLANE - n_new), (0, 0))
    qp, kp, vp = (jnp.pad(a, padr) for a in (q, k_new, v_new))

    n_pool = cache_k.shape[0]
    ck, cv = (c.reshape(n_pool, page * heads, dv) for c in (cache_k, cache_v))

    def page_spec(p_i):
        return pl.BlockSpec((None, page * heads, dv),
                            lambda i, j, pt, s: (pt[i * n_pages + j * pp + p_i], 0, 0))

    row_spec = pl.BlockSpec((None, SUBLANE, hd), lambda i, j, pt, s: (i, 0, 0))
    out = pl.pallas_call(
        functools.partial(_attn_decode_kernel, pp=pp, heads=heads, page=page, n_new=n_new,
                          past_len=n_pages * page, lam_init=lam_init),
        grid_spec=pltpu.PrefetchScalarGridSpec(
            num_scalar_prefetch=2,
            grid=(b, n_pages // pp),
            in_specs=[row_spec, row_spec, row_spec,
                      pl.BlockSpec(lam_vecs.shape, lambda i, j, pt, s: (0, 0)),
                      pl.BlockSpec((1, dv), lambda i, j, pt, s: (0, 0))]
                     + [page_spec(p_i) for p_i in range(pp)] * 2,
            out_specs=row_spec,
            scratch_shapes=[pltpu.VMEM((heads * 2 * SUBLANE, 1), F32),
                            pltpu.VMEM((heads * 2 * SUBLANE, 1), F32),
                            pltpu.VMEM((heads * 2 * SUBLANE, dv), F32)],
        ),
        out_shape=jax.ShapeDtypeStruct((b, SUBLANE, hd), F32),
        compiler_params=_params("parallel", "arbitrary"),
        name="attention_decode",
    )(page_table.reshape(-1), slopes, qp, kp, vp, lam_vecs, subln_g.reshape(1, dv),
      *([ck] * pp), *([cv] * pp))
    return out[:, :n_new]


def _router_kernel(x_ref, g_ref, wr_ref, br_ref, h_ref, idx_ref, gate_ref, cend_ref, run_ref):
    i = pl.program_id(0)
    tm = x_ref.shape[0]

    @pl.when(i == 0)
    def _():
        run_ref[...] = jnp.zeros(run_ref.shape, F32)

    h = _rms(x_ref[...], g_ref[...])
    h_ref[...] = h.astype(BF16)
    logits = lax.dot_general(wr_ref[...], h, (((1,), (1,)), ((), ())), precision=lax.Precision.HIGHEST,
                             preferred_element_type=F32) + br_ref[...]
    ne = logits.shape[0]
    eid = lax.broadcasted_iota(I32, (ne, tm), 0)
    eidf = eid.astype(F32)
    v0 = jnp.max(logits, axis=0, keepdims=True)
    i0 = jnp.min(jnp.where(logits == v0, eidf, float(ne)), axis=0, keepdims=True).astype(I32)
    rest = jnp.where(eid == i0, -jnp.inf, logits)
    v1 = jnp.max(rest, axis=0, keepdims=True)
    i1 = jnp.min(jnp.where(rest == v1, eidf, float(ne)), axis=0, keepdims=True).astype(I32)
    e = jnp.exp(v1 - v0)
    g0 = 1.0 / (1.0 + e)
    g1 = e / (1.0 + e)
    sel0 = eid == i0
    sel1 = eid == i1
    assign = (sel0 | sel1).astype(BF16)
    before = (lax.broadcasted_iota(I32, (tm, tm), 0) < lax.broadcasted_iota(I32, (tm, tm), 1)).astype(BF16)
    rank = _dot(assign, before) + run_ref[...]
    r0 = jnp.sum(jnp.where(sel0, rank, 0.0), axis=0, keepdims=True).astype(I32)
    r1 = jnp.sum(jnp.where(sel1, rank, 0.0), axis=0, keepdims=True).astype(I32)
    idx_ref[...] = jnp.where(eid == 0, i0, jnp.where(eid == 1, i1, jnp.where(eid == 2, r0,
                             jnp.where(eid == 3, r1, 0))))
    gate_ref[...] = jnp.where(eid == 0, g0, jnp.where(eid == 1, g1, 0.0))
    run = run_ref[...] + jnp.sum(assign.astype(F32), axis=1, keepdims=True)
    run_ref[...] = run
    cend_ref[...] = jnp.broadcast_to(run, cend_ref.shape)


def moe_route(x, g, w_router, b_router, tc=MOE_CHUNK):
    t, d = x.shape
    ne = w_router.shape[1]
    assert t % tc == 0 and ne == SUBLANE
    nchunk = t // tc
    return pl.pallas_call(
        _router_kernel,
        grid=(nchunk,),
        in_specs=[pl.BlockSpec((tc, d), lambda i: (i, 0)), _resident((1, d)),
                  _resident((ne, d)), _resident((ne, 1))],
        out_specs=[pl.BlockSpec((tc, d), lambda i: (i, 0)),
                   pl.BlockSpec((ne, tc), lambda i: (0, i)),
                   pl.BlockSpec((ne, tc), lambda i: (0, i)),
                   pl.BlockSpec((None, ne, LANE), lambda i: (i, 0, 0))],
        out_shape=[jax.ShapeDtypeStruct((t, d), BF16), jax.ShapeDtypeStruct((ne, t), I32),
                   jax.ShapeDtypeStruct((ne, t), F32), jax.ShapeDtypeStruct((nchunk, ne, LANE), F32)],
        scratch_shapes=[pltpu.VMEM((ne, 1), F32)],
        compiler_params=_params("arbitrary"),
        name="moe_route",
    )(x, g.reshape(1, d), w_router.T, b_router.reshape(ne, 1))


def _slot_rows(idx_ref, off_ref, slot0, ne):
    idx = idx_ref[...]
    e0, e1, r0, r1 = idx[0:1], idx[1:2], idx[2:3], idx[3:4]
    o0 = jnp.zeros_like(e0)
    o1 = jnp.zeros_like(e1)
    for k in range(ne):
        o0 = jnp.where(e0 == k, off_ref[k], o0)
        o1 = jnp.where(e1 == k, off_ref[k], o1)
    return o0 + r0 - slot0, o1 + r1 - slot0


def _moe_ffn_kernel(wt_ref, wc_ref, wf_ref, te_ref, off_ref, idx_ref, h_ref, wg_ref, wu_ref, wd_ref,
                    o_ref, acc_ref, *, ts, ne):
    w = pl.program_id(0)
    flags = wf_ref[w]

    @pl.when((flags & 2) != 0)
    def _():
        acc_ref[...] = jnp.zeros(acc_ref.shape, F32)

    @pl.when((flags & 1) != 0)
    def _():
        sa, sb = _slot_rows(idx_ref, off_ref, wt_ref[w] * ts, ne)
        rows = lax.broadcasted_iota(I32, (ts, idx_ref.shape[1]), 0)
        pick = ((rows == sa) | (rows == sb)).astype(BF16)
        acc_ref[...] += _dot(pick, h_ref[...])

    @pl.when((flags & 4) != 0)
    def _():
        xb = acc_ref[...].astype(BF16)
        a = _dot(xb, wg_ref[...])
        u = _dot(xb, wu_ref[...])
        o_ref[...] = _dot((_silu(a) * u).astype(BF16), wd_ref[...]).astype(BF16)


def _moe_combine_kernel(wc_ref, wt_ref, wf_ref, off_ref, idx_ref, gate_ref, x_ref, es_ref, o_ref, acc_ref,
                        *, ts, ne):
    w = pl.program_id(0)
    flags = wf_ref[w]

    @pl.when((flags & 2) != 0)
    def _():
        acc_ref[...] = x_ref[...]

    @pl.when((flags & 1) != 0)
    def _():
        sa, sb = _slot_rows(idx_ref, off_ref, wt_ref[w] * ts, ne)
        gt = gate_ref[...]
        rows = lax.broadcasted_iota(I32, (ts, idx_ref.shape[1]), 0)
        wgt = (jnp.where(rows == sa, gt[0:1], 0.0) + jnp.where(rows == sb, gt[1:2], 0.0)).astype(BF16)
        acc_ref[...] += _dot_tn(wgt, es_ref[...])

    @pl.when((flags & 4) != 0)
    def _():
        o_ref[...] = acc_ref[...]


def _work_lists(cend, t, tc, ts, ne):
    nchunk = t // tc
    nt_max = (2 * t) // ts + ne
    w_max = nt_max + ne * nchunk
    cend = cend.astype(I32)
    cstart = jnp.concatenate([jnp.zeros((1, ne), I32), cend[:-1]], axis=0)
    cnt = cend[-1]
    tiles = (cnt + ts - 1) // ts
    tile_end = jnp.cumsum(tiles)
    off = (tile_end - tiles) * ts
    n_tiles = tile_end[-1]
    tile_ids = jnp.arange(nt_max, dtype=I32)

    def count_le(sorted_vals, queries):
        return jnp.sum(sorted_vals[None, :] <= queries[:, None], axis=1).astype(I32)

    te = jnp.minimum(count_le(tile_end, tile_ids), ne - 1)

    def flatten(counts, w_total):
        ends = jnp.cumsum(counts)
        total = ends[-1]
        wi = jnp.minimum(jnp.arange(w_total, dtype=I32), total - 1)
        owner = count_le(ends, wi)
        local = wi - (ends[owner] - counts[owner])
        valid = jnp.arange(w_total, dtype=I32) < total
        return owner, local, valid

    k0 = tile_ids * ts - off[te]
    k1 = jnp.minimum(k0 + ts, cnt[te])
    ce_t = cend[:, te]
    cs_t = cstart[:, te]
    c_lo = jnp.sum(ce_t <= k0[None, :], axis=0).astype(I32)
    c_hi = (nchunk - 1 - jnp.sum(cs_t >= k1[None, :], axis=0)).astype(I32)
    active = tile_ids < n_tiles
    n_items = jnp.where(active, c_hi - c_lo + 1, 0)
    owner, local, valid = flatten(n_items, w_max)
    f_tile, f_chunk = owner, c_lo[owner] + local
    first = local == 0
    last = local == n_items[owner] - 1
    f_flags = (valid * (1 + 2 * first + 4 * last)).astype(I32)

    has = cend > cstart
    s_lo = (off[None, :] + cstart) // ts
    s_hi = (off[None, :] + cend - 1) // ts
    n_ce = jnp.where(has, s_hi - s_lo + 1, 0).reshape(-1)
    owner, local, valid = flatten(n_ce, w_max)
    c_chunk = owner // ne
    c_tile = s_lo.reshape(-1)[owner] + local
    per_chunk = jnp.sum(n_ce.reshape(nchunk, ne), axis=1)
    chunk_end = jnp.cumsum(per_chunk)
    wi = jnp.minimum(jnp.arange(w_max, dtype=I32), chunk_end[-1] - 1)
    first = wi == (chunk_end - per_chunk)[c_chunk]
    last = wi == chunk_end[c_chunk] - 1
    c_flags = (valid * (1 + 2 * first + 4 * last)).astype(I32)
    return (f_tile.astype(I32), f_chunk.astype(I32), f_flags, te, off.astype(I32),
            c_chunk.astype(I32), c_tile.astype(I32), c_flags, nt_max, w_max)


def moe(x, g, w_router, b_router, wg, wu, wd, tc=MOE_CHUNK, ts=MOE_SLOT_TILE):
    t, d = x.shape
    ne, _, f = wg.shape
    tc = min(tc, t)
    h, idx, gate, cend = moe_route(x, g, w_router, b_router, tc)
    (f_tile, f_chunk, f_flags, te, off, c_chunk, c_tile, c_flags, nt_max, w_max) = _work_lists(
        cend[:, :, 0], t, tc, ts, ne)

    sorted_out = pl.pallas_call(
        functools.partial(_moe_ffn_kernel, ts=ts, ne=ne),
        grid_spec=pltpu.PrefetchScalarGridSpec(
            num_scalar_prefetch=5,
            grid=(w_max,),
            in_specs=[
                pl.BlockSpec((ne, tc), lambda w, wt, wc, wf, te_, of: (0, wc[w])),
                pl.BlockSpec((tc, d), lambda w, wt, wc, wf, te_, of: (wc[w], 0)),
                pl.BlockSpec((None, d, f), lambda w, wt, wc, wf, te_, of: (te_[wt[w]], 0, 0)),
                pl.BlockSpec((None, d, f), lambda w, wt, wc, wf, te_, of: (te_[wt[w]], 0, 0)),
                pl.BlockSpec((None, f, d), lambda w, wt, wc, wf, te_, of: (te_[wt[w]], 0, 0)),
            ],
            out_specs=pl.BlockSpec((ts, d), lambda w, wt, wc, wf, te_, of: (wt[w], 0)),
            scratch_shapes=[pltpu.VMEM((ts, d), F32)],
        ),
        out_shape=jax.ShapeDtypeStruct((nt_max * ts, d), BF16),
        compiler_params=_params("arbitrary"),
        name="moe_ffn",
    )(f_tile, f_chunk, f_flags, te, off, idx, h, wg, wu, wd)

    return pl.pallas_call(
        functools.partial(_moe_combine_kernel, ts=ts, ne=ne),
        grid_spec=pltpu.PrefetchScalarGridSpec(
            num_scalar_prefetch=4,
            grid=(w_max,),
            in_specs=[
                pl.BlockSpec((ne, tc), lambda w, wc, wt, wf, of: (0, wc[w])),
                pl.BlockSpec((ne, tc), lambda w, wc, wt, wf, of: (0, wc[w])),
                pl.BlockSpec((tc, d), lambda w, wc, wt, wf, of: (wc[w], 0)),
                pl.BlockSpec((ts, d), lambda w, wc, wt, wf, of: (wt[w], 0)),
            ],
            out_specs=pl.BlockSpec((tc, d), lambda w, wc, wt, wf, of: (wc[w], 0)),
            scratch_shapes=[pltpu.VMEM((tc, d), F32)],
        ),
        out_shape=jax.ShapeDtypeStruct((t, d), F32),
        compiler_params=_params("arbitrary"),
        name="moe_combine",
    )(c_chunk, c_tile, c_flags, off, idx, gate, x, sorted_out)


def _lambda_init(layer):
    return 0.8 - 0.6 * math.exp(-0.3 * layer)


def kernel(x_prompt, x_sample, state_ssm, state_conv, cache_k, cache_v, page_table, norm_mix_g, norm_ffn_g, norm_kv_g, norm_final_g, ssm_w_in, ssm_conv_w, ssm_conv_b, ssm_dt_bias, ssm_a_log, ssm_d, ssm_norm_g, ssm_w_out, kv_w_k, kv_w_v, att_w_q, att_lam_q1, att_lam_k1, att_lam_q2, att_lam_k2, att_subln_g, att_w_o, ffn_w_gate, ffn_w_up, ffn_w_down, moe_w_router, moe_b_router, moe_w_gate, moe_w_up, moe_w_down):
    depth, d_model = norm_mix_g.shape
    n_a = ssm_w_in.shape[0]
    heads_ssm = ssm_a_log.shape[1]
    d_inner = heads_ssm * SSM_HEAD_DIM
    conv_dim = ssm_conv_w.shape[2]
    att_dv = att_subln_g.shape[1]
    att_heads = kv_w_v.shape[1] // att_dv
    slopes = 2.0 ** (-8.0 * jnp.arange(1, att_heads + 1, dtype=F32) / att_heads)

    in_dim = ssm_w_in.shape[2]
    w_in = jnp.pad(ssm_w_in, ((0, 0), (0, 0), (0, d_inner + conv_dim + LANE - in_dim))).astype(BF16)
    w_out = ssm_w_out.astype(BF16)
    w_k, w_v = kv_w_k.astype(BF16), kv_w_v.astype(BF16)
    w_q = att_w_q.astype(BF16)
    w_o = att_w_o.astype(BF16)
    f_gate, f_up, f_down = ffn_w_gate.astype(BF16), ffn_w_up.astype(BF16), ffn_w_down.astype(BF16)
    m_gate, m_up, m_down = moe_w_gate.astype(BF16), moe_w_up.astype(BF16), moe_w_down.astype(BF16)
    lam_vecs = jnp.stack([att_lam_q1, att_lam_k1, att_lam_q2, att_lam_k2], axis=1)

    def trunk(x3, ssm0, conv0, past):
        b, l, _ = x3.shape
        x = x3.reshape(b * l, d_model)
        ssm_new, conv_new = [], []
        k_new = v_new = k_att = v_att = None
        for layer in range(depth):
            if layer < n_a:
                zx = linear(x, w_in[layer], g=norm_mix_g[layer]).reshape(b, l, -1)
                valid = SSD_CHUNK if l % SSD_CHUNK == 0 else l
                y, s, cv = ssd_mixer(zx, conv0[layer], ssm0[layer].reshape(b, d_inner, SSM_STATE),
                                     ssm_conv_w[layer], ssm_conv_b[layer], ssm_dt_bias[layer],
                                     ssm_a_log[layer], ssm_d[layer], ssm_norm_g[layer], valid)
                ssm_new.append(s.reshape(b, heads_ssm, SSM_HEAD_DIM, SSM_STATE))
                conv_new.append(cv)
                x = linear(y.reshape(b * l, d_inner), w_out[layer], residual=x)
            else:
                j = layer - n_a
                lam_init = _lambda_init(layer)
                if past is None:
                    qt = linear(x, w_q[j], g=norm_mix_g[layer], scale=ATT_HEAD_DIM ** -0.5, outs=("bf16_t",),
                                tm=ATT_TILE)
                    o = attention_prompt(qt.reshape(b, l // ATT_TILE, -1, ATT_TILE), k_att, v_att,
                                         lam_vecs[j], att_subln_g[j], slopes, lam_init)
                else:
                    q = linear(x, w_q[j], g=norm_mix_g[layer])
                    o = attention_decode(q.reshape(b, l, -1), k_att, v_att, past[0], past[1], page_table,
                                         lam_vecs[j], att_subln_g[j], slopes, lam_init)
                x = linear(o.reshape(b * l, -1), w_o[j], residual=x)
            i = layer // 2
            if layer % 2 == 0:
                x = ffn(x, norm_ffn_g[layer], f_gate[i], f_up[i], f_down[i])
            else:
                x = moe(x, norm_ffn_g[layer], moe_w_router[i], moe_b_router[i], m_gate[i], m_up[i], m_down[i])
            if layer == n_a - 1:
                if past is None:
                    assert l % ATT_TILE == 0
                    k_new, k_att = linear(x, w_k, g=norm_kv_g, outs=("f32", "bf16"))
                    v_new, v_att = linear(x, w_v, g=norm_kv_g, outs=("f32", "bf16_t"), tm=ATT_TILE)
                    k_att = k_att.reshape(b, l, -1)
                    v_att = v_att.reshape(b, l // ATT_TILE, -1, ATT_TILE)
                else:
                    k_att = k_new = linear(x, w_k, g=norm_kv_g).reshape(b, l, -1)
                    v_att = v_new = linear(x, w_v, g=norm_kv_g).reshape(b, l, -1)
        y = rmsnorm(x, norm_final_g).reshape(b, l, d_model)
        return (y, jnp.stack(ssm_new), jnp.stack(conv_new),
                k_new.reshape(b, l, att_heads, att_dv), v_new.reshape(b, l, att_heads, att_dv))

    bp = x_prompt.shape[0]
    ssm0_p = jnp.zeros((n_a, bp, heads_ssm, SSM_HEAD_DIM, SSM_STATE), F32)
    conv0_p = jnp.zeros((n_a, bp, CONV_WIDTH - 1, conv_dim), F32)
    y_p, ssm_p, conv_p, k_p, v_p = trunk(x_prompt, ssm0_p, conv0_p, None)

    y_s, ssm_s, conv_s, k_s, v_s = trunk(x_sample, state_ssm, state_conv, (cache_k, cache_v))
    return (y_p, y_s, ssm_p, conv_p, k_p, v_p, ssm_s, conv_s, k_s, v_s)
```

```python
import functools
import math

import jax
import jax.numpy as jnp
from jax import lax
from jax.experimental import pallas as pl
from jax.experimental.pallas import tpu as pltpu

F32 = jnp.float32
BF16 = jnp.bfloat16
I32 = jnp.int32

RMS_EPS = 1e-5
SSM_HEAD_DIM = 64
SSM_GROUPS = 8
SSM_STATE = 128
CONV_WIDTH = 4
SSD_CHUNK = 128
ATT_HEAD_DIM = 64
N_EXPERTS = 8
LANE = 128
SUBLANE = 8
VMEM_LIMIT = 56 * 1024 * 1024
NEG = -1e30

TOKEN_TILE = 256
MOE_SLOT_TILE = 256
MOE_CHUNK = 1024
ATT_TILE = 256
PAGES_PER_STEP = 16


def _params(*sem):
    return pltpu.CompilerParams(dimension_semantics=sem, vmem_limit_bytes=VMEM_LIMIT)


def _resident(shape):
    nd = len(shape)
    return pl.BlockSpec(shape, lambda *_: (0,) * nd, pipeline_mode=pl.Buffered(1))


def _rms(x, g):
    ms = jnp.mean(x * x, axis=-1, keepdims=True)
    return x * lax.rsqrt(ms + RMS_EPS) * g


def _silu(x):
    return (0.5 * x) * (1.0 + jnp.tanh(0.5 * x))


def _dot(a, b):
    return jnp.dot(a, b, preferred_element_type=F32)


def _dot_nt(a, b):
    return lax.dot_general(a, b, (((1,), (1,)), ((), ())), preferred_element_type=F32)


def _dot_tn(a, b):
    return lax.dot_general(a, b, (((0,), (0,)), ((), ())), preferred_element_type=F32)


def _split2(x):
    hi = x.astype(BF16)
    lo = (x - hi.astype(F32)).astype(BF16)
    return hi, lo


def _split3(x):
    hi = x.astype(BF16)
    r = x - hi.astype(F32)
    mid = r.astype(BF16)
    lo = (r - mid.astype(F32)).astype(BF16)
    return hi, mid, lo


def _linear_kernel(*refs, norm, residual, scale, outs):
    it = iter(refs)
    x_ref = next(it)
    g_ref = next(it) if norm else None
    w_ref = next(it)
    r_ref = next(it) if residual else None
    x = x_ref[...].astype(F32)
    if norm:
        x = _rms(x, g_ref[...])
    y = _dot(x.astype(BF16), w_ref[...])
    if residual:
        y = y + r_ref[...]
    if scale is not None:
        y = y * scale
    for kind in outs:
        o_ref = next(it)
        if kind == "f32":
            o_ref[...] = y
        elif kind == "bf16":
            o_ref[...] = y.astype(BF16)
        else:
            o_ref[...] = jnp.transpose(y).astype(BF16)


def linear(x, w, g=None, residual=None, scale=None, outs=("f32",), tm=TOKEN_TILE):
    t, k = x.shape
    n = w.shape[1]
    tm = min(tm, t)
    assert t % tm == 0
    args = [x]
    specs = [pl.BlockSpec((tm, k), lambda i: (i, 0))]
    if g is not None:
        args.append(g.reshape(1, k))
        specs.append(_resident((1, k)))
    args.append(w)
    specs.append(_resident((k, n)))
    if residual is not None:
        args.append(residual)
        specs.append(pl.BlockSpec((tm, n), lambda i: (i, 0)))
    out_specs, out_shapes = [], []
    for kind in outs:
        if kind == "bf16_t":
            out_specs.append(pl.BlockSpec((None, n, tm), lambda i: (i, 0, 0)))
            out_shapes.append(jax.ShapeDtypeStruct((t // tm, n, tm), BF16))
        else:
            out_specs.append(pl.BlockSpec((tm, n), lambda i: (i, 0)))
            out_shapes.append(jax.ShapeDtypeStruct((t, n), F32 if kind == "f32" else BF16))
    res = pl.pallas_call(
        functools.partial(_linear_kernel, norm=g is not None, residual=residual is not None,
                          scale=scale, outs=tuple(outs)),
        grid=(t // tm,),
        in_specs=specs,
        out_specs=out_specs,
        out_shape=out_shapes,
        compiler_params=_params("parallel"),
        name="linear",
    )(*args)
    return res[0] if len(outs) == 1 else res


def _ffn_kernel(x_ref, g_ref, wg_ref, wu_ref, wd_ref, o_ref):
    x = x_ref[...]
    h = _rms(x, g_ref[...]).astype(BF16)
    a = _dot(h, wg_ref[...])
    u = _dot(h, wu_ref[...])
    act = (_silu(a) * u).astype(BF16)
    o_ref[...] = x + _dot(act, wd_ref[...])


def ffn(x, g, wg, wu, wd, tm=TOKEN_TILE):
    t, d = x.shape
    f = wg.shape[1]
    tm = min(tm, t)
    assert t % tm == 0
    return pl.pallas_call(
        _ffn_kernel,
        grid=(t // tm,),
        in_specs=[pl.BlockSpec((tm, d), lambda i: (i, 0)), _resident((1, d)),
                  _resident((d, f)), _resident((d, f)), _resident((f, d))],
        out_specs=pl.BlockSpec((tm, d), lambda i: (i, 0)),
        out_shape=jax.ShapeDtypeStruct((t, d), F32),
        compiler_params=_params("parallel"),
        name="ffn",
    )(x, g.reshape(1, d), wg, wu, wd)


def _rmsnorm_kernel(x_ref, g_ref, o_ref):
    o_ref[...] = _rms(x_ref[...], g_ref[...])


def rmsnorm(x, g, tm=TOKEN_TILE):
    t, d = x.shape
    tm = min(tm, t)
    assert t % tm == 0
    return pl.pallas_call(
        _rmsnorm_kernel,
        grid=(t // tm,),
        in_specs=[pl.BlockSpec((tm, d), lambda i: (i, 0)), _resident((1, d))],
        out_specs=pl.BlockSpec((tm, d), lambda i: (i, 0)),
        out_shape=jax.ShapeDtypeStruct((t, d), F32),
        compiler_params=_params("parallel"),
        name="rmsnorm",
    )(x, g.reshape(1, d))


def _ssd_kernel(*refs, q, rows, valid, d_inner, heads, fused, has_init, has_prev):
    it = iter(refs)
    src_ref = next(it)
    if fused:
        gmix_ref, win_ref, wout_ref = next(it), next(it), next(it)
    if has_init:
        conv0_ref, ssm0_ref = next(it), next(it)
    cw_ref, cb_ref, dtb_ref, alog_ref, dsk_ref, ng_ref, exp_ref = (next(it) for _ in range(7))
    if has_prev:
        next(it), next(it)
    out_ref, sfin_ref, cnew_ref, xp_ref, stage_ref = (next(it) for _ in range(5))
    ybuf_ref = next(it) if fused else None

    c = pl.program_id(1)
    nc = pl.num_programs(1)
    kp = SSD_CHUNK
    gn = SSM_GROUPS * SSM_STATE
    hpg = heads // SSM_GROUPS
    gw = hpg * SSM_HEAD_DIM
    halo = CONV_WIDTH - 1

    @pl.when(c == 0)
    def _():
        xp_ref[0:SUBLANE, :] = jnp.zeros((SUBLANE, xp_ref.shape[1]), F32)
        if has_init:
            sfin_ref[...] = ssm0_ref[...]
            xp_ref[SUBLANE - halo:SUBLANE, :] = conv0_ref[...]
        else:
            sfin_ref[...] = jnp.zeros(sfin_ref.shape, F32)

    if rows < q:
        stage_ref[...] = jnp.zeros(stage_ref.shape, F32)
        stage_ref[0:rows, :] = src_ref[...]
        src = stage_ref[...]
    else:
        src = src_ref[...]
    if fused:
        zx = _dot(_rms(src, gmix_ref[...]).astype(BF16), win_ref[...])
    else:
        zx = src
    z = zx[:, :d_inner]
    xbc = zx[:, d_inner:2 * d_inner + 2 * gn]
    dtr = zx[:, 2 * d_inner + 2 * gn:]

    xp_ref[SUBLANE:SUBLANE + q, :] = xbc
    xall = xp_ref[...]
    conv = cb_ref[...] + xall[SUBLANE:] * cw_ref[halo:halo + 1, :]
    for shift in range(1, CONV_WIDTH):
        k = halo - shift
        conv = conv + pltpu.roll(xall, shift, 0)[SUBLANE:] * cw_ref[k:k + 1, :]
    tail = xp_ref[SUBLANE + valid - halo:SUBLANE + valid, :]
    xp_ref[SUBLANE - halo:SUBLANE, :] = tail

    @pl.when(c == nc - 1)
    def _():
        cnew_ref[...] = tail

    xa = _silu(conv)
    xs = xa[:, :d_inner]
    bm = xa[:, d_inner:d_inner + gn]
    cm = xa[:, d_inner + gn:]

    def pad(v):
        if q == kp:
            return v
        return jnp.concatenate([v, jnp.zeros((kp - q, v.shape[1]), v.dtype)], axis=0)

    row = lax.broadcasted_iota(I32, (q, kp), 0)
    col = lax.broadcasted_iota(I32, (q, kp), 1)
    causal = col <= row

    dtv = dtr + dtb_ref[...]
    dt = jnp.maximum(dtv, 0.0) + jnp.log1p(jnp.exp(-jnp.abs(dtv)))
    if valid < q:
        dt = jnp.where(lax.broadcasted_iota(I32, (q, LANE), 0) < valid, dt, 0.0)
    adt = dt * (-jnp.exp(alog_ref[...]))
    tril = causal.astype(BF16)
    cs = sum(_dot(tril, part) for part in _split3(pad(adt)))
    cs_t = jnp.transpose(pad(cs))
    cs_last = cs[q - 1:q, :]
    e_last = jnp.exp(cs_last)

    stacked = jnp.concatenate([dt, jnp.exp(cs), jnp.exp(cs_last - cs)], axis=0)
    hi, lo = _split2(stacked)
    wide = _dot(hi, exp_ref[...]) + _dot(lo, exp_ref[...])
    dt_w, ecs_w, dout_w = wide[0:q], wide[q:2 * q], wide[2 * q:3 * q]
    xdt = xs * dt_w
    xdtd = pad(xdt * dout_w).astype(BF16)
    xdt_b = pad(xdt).astype(BF16)
    bm_p = pad(bm).astype(BF16)
    cm_b = cm.astype(BF16)
    lane_g = lax.broadcasted_iota(I32, (kp, gw), 1) // SSM_HEAD_DIM
    head_mask = [(lane_g == r).astype(BF16) for r in range(hpg)]

    for g in range(SSM_GROUPS):
        b_g = bm_p[:, g * SSM_STATE:(g + 1) * SSM_STATE]
        c_g = cm_b[:, g * SSM_STATE:(g + 1) * SSM_STATE]
        cb = _dot_nt(c_g, b_g)
        xg = xdt_b[:, g * gw:(g + 1) * gw]
        m_parts, x_parts, scale_parts = [], [], []
        for r in range(hpg):
            h = g * hpg + r
            seg = cs[:, h:h + 1] - cs_t[h:h + 1, :]
            decay = jnp.exp(jnp.where(causal, seg, NEG))
            m_parts.append((cb * decay).astype(BF16))
            x_parts.append(xg * head_mask[r])
            scale_parts.append(jnp.broadcast_to(e_last[:, h:h + 1], (SSM_HEAD_DIM, SSM_STATE)))
        y_diag = _dot(jnp.concatenate(m_parts, axis=1), jnp.concatenate(x_parts, axis=0))
        s_g = sfin_ref[g * gw:(g + 1) * gw, :]
        y_off = _dot_nt(c_g, s_g.astype(BF16)) * ecs_w[:, g * gw:(g + 1) * gw]
        new = _dot_tn(xdtd[:, g * gw:(g + 1) * gw], b_g)
        sfin_ref[g * gw:(g + 1) * gw, :] = s_g * jnp.concatenate(scale_parts, axis=0) + new
        y = y_diag + y_off + dsk_ref[:, g * gw:(g + 1) * gw] * xs[:, g * gw:(g + 1) * gw]
        y = y * _silu(z[:, g * gw:(g + 1) * gw])
        y = y * lax.rsqrt(jnp.mean(y * y, axis=-1, keepdims=True) + RMS_EPS)
        y = y * ng_ref[:, g * gw:(g + 1) * gw]
        if fused:
            ybuf_ref[:, g * gw:(g + 1) * gw] = y.astype(BF16)
        else:
            out_ref[:, g * gw:(g + 1) * gw] = y

    if fused:
        out_ref[...] = src + _dot(ybuf_ref[...], wout_ref[...])


def ssd_mixer(src, layer, n_layers, conv0_all, ssm0_all, prev, conv_w, conv_b, dt_bias, a_log, d_skip,
              norm_g, valid, fuse=None):
    b, l, width = src.shape
    heads = a_log.shape[0]
    d_inner = heads * SSM_HEAD_DIM
    conv_dim = conv_w.shape[1]
    if l % SSD_CHUNK == 0:
        q = rows = SSD_CHUNK
        nc = l // SSD_CHUNK
        assert valid == SSD_CHUNK
    else:
        assert l < SSD_CHUNK
        rows, nc = l, 1
        q = -(-l // SUBLANE) * SUBLANE
    padh = LANE - heads
    expand = jnp.repeat(jnp.eye(LANE, heads, dtype=BF16), SSM_HEAD_DIM, axis=1)
    halo = CONV_WIDTH - 1

    args = [src]
    specs = [pl.BlockSpec((None, rows, width), lambda i, j: (i, j, 0))]
    if fuse is not None:
        g_mix, w_in, w_out = fuse
        assert w_in.shape == (width, d_inner + conv_dim + LANE)
        args += [g_mix.reshape(1, width), w_in, w_out]
        specs += [_resident((1, width)), _resident(w_in.shape), _resident(w_out.shape)]
        out_width = w_out.shape[1]
    else:
        assert width == d_inner + conv_dim + LANE
        out_width = d_inner
    if ssm0_all is not None:
        args += [conv0_all, ssm0_all]
        specs += [pl.BlockSpec((None, None, halo, conv_dim), lambda i, j: (layer, i, 0, 0)),
                  pl.BlockSpec((None, None, d_inner, SSM_STATE), lambda i, j: (layer, i, 0, 0))]
    args += [conv_w, conv_b.reshape(1, conv_dim),
             jnp.pad(dt_bias, (0, padh)).reshape(1, LANE), jnp.pad(a_log, (0, padh)).reshape(1, LANE),
             jnp.repeat(d_skip, SSM_HEAD_DIM).reshape(1, d_inner), norm_g.reshape(1, d_inner), expand]
    specs += [_resident((CONV_WIDTH, conv_dim)), _resident((1, conv_dim)), _resident((1, LANE)),
              _resident((1, LANE)), _resident((1, d_inner)), _resident((1, d_inner)),
              _resident((LANE, d_inner))]
    aliases = {}
    if prev is not None:
        aliases = {len(args): 1, len(args) + 1: 2}
        args += list(prev)
        specs += [pl.BlockSpec(memory_space=pl.ANY), pl.BlockSpec(memory_space=pl.ANY)]
    scratch = [pltpu.VMEM((SUBLANE + q, conv_dim), F32),
               pltpu.VMEM((q, width) if rows < q else (SUBLANE, LANE), F32)]
    if fuse is not None:
        scratch.append(pltpu.VMEM((q, d_inner), BF16))
    kern = functools.partial(_ssd_kernel, q=q, rows=rows, valid=valid, d_inner=d_inner, heads=heads,
                             fused=fuse is not None, has_init=ssm0_all is not None, has_prev=prev is not None)
    out, sfin, cnew = pl.pallas_call(
        kern,
        grid=(b, nc),
        in_specs=specs,
        out_specs=[
            pl.BlockSpec((None, q, out_width), lambda i, j: (i, j, 0)),
            pl.BlockSpec((None, None, d_inner, SSM_STATE), lambda i, j: (layer, i, 0, 0)),
            pl.BlockSpec((None, None, halo, conv_dim), lambda i, j: (layer, i, 0, 0)),
        ],
        out_shape=[
            jax.ShapeDtypeStruct((b, nc * q, out_width), F32),
            jax.ShapeDtypeStruct((n_layers, b, d_inner, SSM_STATE), F32),
            jax.ShapeDtypeStruct((n_layers, b, halo, conv_dim), F32),
        ],
        scratch_shapes=scratch,
        input_output_aliases=aliases,
        compiler_params=_params("parallel", "arbitrary"),
        name="ssd_mixer",
    )(*args)
    return out[:, :l], sfin, cnew


def _lambda(lam_ref, lam_init):
    lv = lam_ref[...]
    s1 = jnp.sum(lv[0:1] * lv[1:2], axis=-1, keepdims=True)
    s2 = jnp.sum(lv[2:3] * lv[3:4], axis=-1, keepdims=True)
    return jnp.exp(s1) - jnp.exp(s2) + lam_init


def _two_branch_q(qv, n):
    lane = lax.broadcasted_iota(I32, qv.shape, 1)
    zero = jnp.zeros_like(qv)
    return jnp.concatenate([jnp.where(lane < ATT_HEAD_DIM, qv, zero),
                            jnp.where(lane >= ATT_HEAD_DIM, qv, zero)], axis=0).astype(BF16)


def _attn_prompt_kernel(slope_ref, qt_ref, k_ref, vt_ref, lam_ref, sg_ref, o_ref, *, tq, lam_init):
    h = pl.program_id(1)
    qi = pl.program_id(2)
    slope = slope_ref[h]
    dv = qt_ref.shape[0]
    qt = qt_ref[...]
    feat = lax.broadcasted_iota(I32, qt.shape, 0)
    zero = jnp.zeros_like(qt)
    q2t = jnp.concatenate([jnp.where(feat < ATT_HEAD_DIM, qt, zero),
                           jnp.where(feat >= ATT_HEAD_DIM, qt, zero)], axis=1)
    cc = lax.broadcasted_iota(I32, (tq, 2 * tq), 1)
    rel = jnp.where(cc >= tq, cc - tq, cc) - lax.broadcasted_iota(I32, (tq, 2 * tq), 0)
    bias = -slope * rel.astype(F32)

    def scores(j):
        start = pl.multiple_of(j * tq, tq)
        return _dot(k_ref[pl.ds(start, tq), :], q2t) + bias

    def consume(j, s, carry, diagonal):
        m, l, acc = carry
        if diagonal:
            s = jnp.where(rel >= 0, s, NEG)
        tile_bias = -slope * ((qi - j) * tq).astype(F32)
        m_new = jnp.maximum(m, jnp.max(s, axis=0, keepdims=True) + tile_bias)
        alpha = jnp.exp(m - m_new)
        p = jnp.exp(s - (m_new - tile_bias))
        l = alpha * l + jnp.sum(p, axis=0, keepdims=True)
        acc = alpha * acc + _dot(vt_ref[j], p.astype(BF16))
        return m_new, l, acc

    m0 = jnp.full((1, 2 * tq), NEG, F32)
    l0 = jnp.zeros((1, 2 * tq), F32)
    a0 = jnp.zeros((dv, 2 * tq), F32)

    def body(j, c):
        s_next = scores(j + 1)
        return consume(j, c[3], c[:3], False) + (s_next,)

    c = lax.fori_loop(0, qi, body, (m0, l0, a0, scores(0)))
    _, l, acc = consume(qi, c[3], c[:3], True)
    o = acc / l
    o = jnp.transpose(o[:, :tq] - _lambda(lam_ref, lam_init) * o[:, tq:])
    o_ref[...] = _rms(o, sg_ref[...]) * (1.0 - lam_init)


def attention_prompt(qt, k, vt, lam_vecs, subln_g, slopes, lam_init):
    b, nq, hd, tq = qt.shape
    l = nq * tq
    dv = subln_g.shape[0]
    heads = hd // dv
    return pl.pallas_call(
        functools.partial(_attn_prompt_kernel, tq=tq, lam_init=lam_init),
        grid_spec=pltpu.PrefetchScalarGridSpec(
            num_scalar_prefetch=1,
            grid=(b, heads, nq),
            in_specs=[
                pl.BlockSpec((None, None, dv, tq), lambda i, h, j, s: (i, j, h, 0)),
                pl.BlockSpec((None, l, dv), lambda i, h, j, s: (i, 0, h)),
                pl.BlockSpec((None, nq, dv, tq), lambda i, h, j, s: (i, 0, h, 0)),
                pl.BlockSpec(lam_vecs.shape, lambda i, h, j, s: (0, 0)),
                pl.BlockSpec((1, dv), lambda i, h, j, s: (0, 0)),
            ],
            out_specs=pl.BlockSpec((None, tq, dv), lambda i, h, j, s: (i, j, h)),
        ),
        out_shape=jax.ShapeDtypeStruct((b, l, hd), F32),
        compiler_params=_params("parallel", "parallel", "arbitrary"),
        name="attention_prompt",
    )(slopes, qt, k, vt, lam_vecs, subln_g.reshape(1, dv))


def _attn_decode_kernel(pt_ref, slope_ref, q_ref, kn_ref, vn_ref, lam_ref, sg_ref, *rest,
                        pp, heads, page, n_new, past_len, lam_init):
    k_refs = rest[:pp]
    v_refs = rest[pp:2 * pp]
    o_ref = rest[2 * pp]
    m_ref, l_ref, acc_ref = rest[2 * pp + 1:]
    j = pl.program_id(1)
    nj = pl.num_programs(1)
    dv = sg_ref.shape[1]
    nr = 2 * SUBLANE
    row = lax.broadcasted_iota(I32, (heads * nr, page), 0)
    col = lax.broadcasted_iota(I32, (heads * nr, page), 1)
    slope_col = jnp.concatenate([jnp.full((nr, 1), slope_ref[h], F32) for h in range(heads)], axis=0)

    @pl.when(j == 0)
    def _():
        m_ref[...] = jnp.full(m_ref.shape, NEG, F32)
        l_ref[...] = jnp.zeros(l_ref.shape, F32)
        acc_ref[...] = jnp.zeros(acc_ref.shape, F32)

    qv = q_ref[...] * (ATT_HEAD_DIM ** -0.5)
    q2 = [_two_branch_q(qv[:, h * dv:(h + 1) * dv], SUBLANE) for h in range(heads)]

    def update(k_of, v_of, bias, n_pg):
        s = jnp.concatenate([jnp.concatenate([_dot_nt(q2[h], k_of(p, h)) for h in range(heads)], axis=0)
                             for p in range(n_pg)], axis=1) + bias
        m = m_ref[...]
        m_new = jnp.maximum(m, jnp.max(s, axis=-1, keepdims=True))
        alpha = jnp.exp(m - m_new)
        pr = jnp.exp(s - m_new)
        l_ref[...] = alpha * l_ref[...] + jnp.sum(pr, axis=-1, keepdims=True)
        pb = pr.astype(BF16)
        pv = jnp.concatenate(
            [sum(_dot(pb[h * nr:(h + 1) * nr, p * page:(p + 1) * page], v_of(p, h)) for p in range(n_pg))
             for h in range(heads)], axis=0)
        acc_ref[...] = alpha * acc_ref[...] + pv
        m_ref[...] = m_new

    rows_w = lax.broadcasted_iota(I32, (heads * nr, pp * page), 0)
    kpos = (j * (pp * page) + lax.broadcasted_iota(I32, (heads * nr, pp * page), 1)).astype(F32)
    qpos_w = (past_len + (rows_w % SUBLANE)).astype(F32)
    update(lambda p, h: k_refs[p][pl.ds(h, page, stride=heads), :].astype(BF16),
           lambda p, h: v_refs[p][pl.ds(h, page, stride=heads), :].astype(BF16),
           -slope_col * (qpos_w - kpos), pp)

    @pl.when(j == nj - 1)
    def _():
        zeros = jnp.zeros((page - SUBLANE, kn_ref.shape[1]), F32)
        kb = jnp.concatenate([kn_ref[...], zeros], axis=0).astype(BF16)
        vb = jnp.concatenate([vn_ref[...], zeros], axis=0).astype(BF16)
        dist = (row % SUBLANE) - col
        ok = (dist >= 0) & (col < n_new)
        update(lambda p, h: kb[:, h * dv:(h + 1) * dv], lambda p, h: vb[:, h * dv:(h + 1) * dv],
               jnp.where(ok, -slope_col * dist.astype(F32), NEG), 1)
        o = acc_ref[...] / l_ref[...]
        lam = _lambda(lam_ref, lam_init)
        for h in range(heads):
            oh = o[h * nr:h * nr + SUBLANE] - lam * o[h * nr + SUBLANE:(h + 1) * nr]
            o_ref[:, h * dv:(h + 1) * dv] = _rms(oh, sg_ref[...]) * (1.0 - lam_init)


def attention_decode(q, k_new, v_new, cache_k, cache_v, page_table, lam_vecs, subln_g, slopes, lam_init):
    b, n_new, hd = q.shape
    dv = subln_g.shape[0]
    heads = hd // dv
    n_pages = page_table.shape[1]
    page = cache_k.shape[1]
    pp = math.gcd(PAGES_PER_STEP, n_pages)
    assert n_new <= SUBLANE
    padr = ((0, 0), (0, SUBLANE - n_new), (0, 0))
    qp, kp, vp = (jnp.pad(a, padr) for a in (q, k_new, v_new))

    n_pool = cache_k.shape[0]
    ck, cv = (c.reshape(n_pool, page * heads, dv) for c in (cache_k, cache_v))

    def page_spec(p_i):
        return pl.BlockSpec((None, page * heads, dv),
                            lambda i, j, pt, s: (pt[i * n_pages + j * pp + p_i], 0, 0))

    row_spec = pl.BlockSpec((None, SUBLANE, hd), lambda i, j, pt, s: (i, 0, 0))
    out = pl.pallas_call(
        functools.partial(_attn_decode_kernel, pp=pp, heads=heads, page=page, n_new=n_new,
                          past_len=n_pages * page, lam_init=lam_init),
        grid_spec=pltpu.PrefetchScalarGridSpec(
            num_scalar_prefetch=2,
            grid=(b, n_pages // pp),
            in_specs=[row_spec, row_spec, row_spec,
                      pl.BlockSpec(lam_vecs.shape, lambda i, j, pt, s: (0, 0)),
                      pl.BlockSpec((1, dv), lambda i, j, pt, s: (0, 0))]
                     + [page_spec(p_i) for p_i in range(pp)] * 2,
            out_specs=row_spec,
            scratch_shapes=[pltpu.VMEM((heads * 2 * SUBLANE, 1), F32),
                            pltpu.VMEM((heads * 2 * SUBLANE, 1), F32),
                            pltpu.VMEM((heads * 2 * SUBLANE, dv), F32)],
        ),
        out_shape=jax.ShapeDtypeStruct((b, SUBLANE, hd), F32),
        compiler_params=_params("parallel", "arbitrary"),
        name="attention_decode",
    )(page_table.reshape(-1), slopes, qp, kp, vp, lam_vecs, subln_g.reshape(1, dv),
      *([ck] * pp), *([cv] * pp))
    return out[:, :n_new]


def _router_kernel(x_ref, g_ref, wr_ref, br_ref, h_ref, idx_ref, gate_ref, cend_ref, run_ref):
    i = pl.program_id(0)
    tm = x_ref.shape[0]

    @pl.when(i == 0)
    def _():
        run_ref[...] = jnp.zeros(run_ref.shape, F32)

    h = _rms(x_ref[...], g_ref[...])
    h_ref[...] = h.astype(BF16)
    logits = lax.dot_general(wr_ref[...], h, (((1,), (1,)), ((), ())), precision=lax.Precision.HIGHEST,
                             preferred_element_type=F32) + br_ref[...]
    ne = logits.shape[0]
    eid = lax.broadcasted_iota(I32, (ne, tm), 0)
    eidf = eid.astype(F32)
    v0 = jnp.max(logits, axis=0, keepdims=True)
    i0 = jnp.min(jnp.where(logits == v0, eidf, float(ne)), axis=0, keepdims=True).astype(I32)
    rest = jnp.where(eid == i0, -jnp.inf, logits)
    v1 = jnp.max(rest, axis=0, keepdims=True)
    i1 = jnp.min(jnp.where(rest == v1, eidf, float(ne)), axis=0, keepdims=True).astype(I32)
    e = jnp.exp(v1 - v0)
    g0 = 1.0 / (1.0 + e)
    g1 = e / (1.0 + e)
    sel0 = eid == i0
    sel1 = eid == i1
    assign = (sel0 | sel1).astype(BF16)
    before = (lax.broadcasted_iota(I32, (tm, tm), 0) < lax.broadcasted_iota(I32, (tm, tm), 1)).astype(BF16)
    rank = _dot(assign, before) + run_ref[...]
    r0 = jnp.sum(jnp.where(sel0, rank, 0.0), axis=0, keepdims=True).astype(I32)
    r1 = jnp.sum(jnp.where(sel1, rank, 0.0), axis=0, keepdims=True).astype(I32)
    idx_ref[...] = jnp.where(eid == 0, i0, jnp.where(eid == 1, i1, jnp.where(eid == 2, r0,
                             jnp.where(eid == 3, r1, 0))))
    gate_ref[...] = jnp.where(eid == 0, g0, jnp.where(eid == 1, g1, 0.0))
    run = run_ref[...] + jnp.sum(assign.astype(F32), axis=1, keepdims=True)
    run_ref[...] = run
    cend_ref[...] = jnp.broadcast_to(run, cend_ref.shape)


def moe_route(x, g, w_router, b_router, tc=MOE_CHUNK):
    t, d = x.shape
    ne = w_router.shape[1]
    assert t % tc == 0 and ne == SUBLANE
    nchunk = t // tc
    return pl.pallas_call(
        _router_kernel,
        grid=(nchunk,),
        in_specs=[pl.BlockSpec((tc, d), lambda i: (i, 0)), _resident((1, d)),
                  _resident((ne, d)), _resident((ne, 1))],
        out_specs=[pl.BlockSpec((tc, d), lambda i: (i, 0)),
                   pl.BlockSpec((ne, tc), lambda i: (0, i)),
                   pl.BlockSpec((ne, tc), lambda i: (0, i)),
                   pl.BlockSpec((None, ne, LANE), lambda i: (i, 0, 0))],
        out_shape=[jax.ShapeDtypeStruct((t, d), BF16), jax.ShapeDtypeStruct((ne, t), I32),
                   jax.ShapeDtypeStruct((ne, t), F32), jax.ShapeDtypeStruct((nchunk, ne, LANE), F32)],
        scratch_shapes=[pltpu.VMEM((ne, 1), F32)],
        compiler_params=_params("arbitrary"),
        name="moe_route",
    )(x, g.reshape(1, d), w_router.T, b_router.reshape(ne, 1))


def _slot_rows(idx_ref, off_ref, slot0, ne):
    idx = idx_ref[...]
    e0, e1, r0, r1 = idx[0:1], idx[1:2], idx[2:3], idx[3:4]
    o0 = jnp.zeros_like(e0)
    o1 = jnp.zeros_like(e1)
    for k in range(ne):
        o0 = jnp.where(e0 == k, off_ref[k], o0)
        o1 = jnp.where(e1 == k, off_ref[k], o1)
    return o0 + r0 - slot0, o1 + r1 - slot0


def _moe_ffn_kernel(wt_ref, wc_ref, wf_ref, te_ref, off_ref, idx_ref, h_ref, wg_ref, wu_ref, wd_ref,
                    o_ref, acc_ref, *, ts, ne):
    w = pl.program_id(0)
    flags = wf_ref[w]

    @pl.when((flags & 2) != 0)
    def _():
        acc_ref[...] = jnp.zeros(acc_ref.shape, F32)

    @pl.when((flags & 1) != 0)
    def _():
        sa, sb = _slot_rows(idx_ref, off_ref, wt_ref[w] * ts, ne)
        rows = lax.broadcasted_iota(I32, (ts, idx_ref.shape[1]), 0)
        pick = ((rows == sa) | (rows == sb)).astype(BF16)
        acc_ref[...] += _dot(pick, h_ref[...])

    @pl.when((flags & 4) != 0)
    def _():
        xb = acc_ref[...].astype(BF16)
        a = _dot(xb, wg_ref[...])
        u = _dot(xb, wu_ref[...])
        o_ref[...] = _dot((_silu(a) * u).astype(BF16), wd_ref[...]).astype(BF16)


def _moe_combine_kernel(wc_ref, wt_ref, wf_ref, off_ref, idx_ref, gate_ref, x_ref, es_ref, o_ref, acc_ref,
                        *, ts, ne):
    w = pl.program_id(0)
    flags = wf_ref[w]

    @pl.when((flags & 2) != 0)
    def _():
        acc_ref[...] = x_ref[...]

    @pl.when((flags & 1) != 0)
    def _():
        sa, sb = _slot_rows(idx_ref, off_ref, wt_ref[w] * ts, ne)
        gt = gate_ref[...]
        rows = lax.broadcasted_iota(I32, (ts, idx_ref.shape[1]), 0)
        wgt = (jnp.where(rows == sa, gt[0:1], 0.0) + jnp.where(rows == sb, gt[1:2], 0.0)).astype(BF16)
        acc_ref[...] += _dot_tn(wgt, es_ref[...])

    @pl.when((flags & 4) != 0)
    def _():
        o_ref[...] = acc_ref[...]


def _work_lists(cend, t, tc, ts, ne):
    nchunk = t // tc
    nt_max = (2 * t) // ts + ne
    w_max = nt_max + ne * nchunk
    cend = cend.astype(I32)
    cstart = jnp.concatenate([jnp.zeros((1, ne), I32), cend[:-1]], axis=0)
    cnt = cend[-1]
    tiles = (cnt + ts - 1) // ts
    tile_end = jnp.cumsum(tiles)
    off = (tile_end - tiles) * ts
    n_tiles = tile_end[-1]
    tile_ids = jnp.arange(nt_max, dtype=I32)

    def count_le(sorted_vals, queries):
        return jnp.sum(sorted_vals[None, :] <= queries[:, None], axis=1).astype(I32)

    te = jnp.minimum(count_le(tile_end, tile_ids), ne - 1)

    def flatten(counts, w_total):
        ends = jnp.cumsum(counts)
        total = ends[-1]
        wi = jnp.clip(jnp.arange(w_total, dtype=I32), 0, jnp.maximum(total - 1, 0))
        owner = jnp.minimum(count_le(ends, wi), counts.shape[0] - 1)
        local = wi - (ends[owner] - counts[owner])
        valid = jnp.arange(w_total, dtype=I32) < total
        return owner, local, valid

    k0 = tile_ids * ts - off[te]
    k1 = jnp.minimum(k0 + ts, cnt[te])
    ce_t = cend[:, te]
    cs_t = cstart[:, te]
    c_lo = jnp.sum(ce_t <= k0[None, :], axis=0).astype(I32)
    c_hi = (nchunk - 1 - jnp.sum(cs_t >= k1[None, :], axis=0)).astype(I32)
    active = tile_ids < n_tiles
    n_items = jnp.where(active, c_hi - c_lo + 1, 0)
    owner, local, valid = flatten(n_items, w_max)
    f_tile, f_chunk = owner, jnp.clip(c_lo[owner] + local, 0, nchunk - 1)
    first = local == 0
    last = local == n_items[owner] - 1
    f_flags = (valid * (1 + 2 * first + 4 * last)).astype(I32)

    has = cend > cstart
    s_lo = (off[None, :] + cstart) // ts
    s_hi = (off[None, :] + cend - 1) // ts
    n_ce = jnp.where(has, s_hi - s_lo + 1, 0).reshape(-1)
    owner, local, valid = flatten(n_ce, w_max)
    c_chunk = owner // ne
    c_tile = jnp.clip(s_lo.reshape(-1)[owner] + local, 0, nt_max - 1)
    per_chunk = jnp.sum(n_ce.reshape(nchunk, ne), axis=1)
    chunk_end = jnp.cumsum(per_chunk)
    wi = jnp.clip(jnp.arange(w_max, dtype=I32), 0, jnp.maximum(chunk_end[-1] - 1, 0))
    first = wi == (chunk_end - per_chunk)[c_chunk]
    last = wi == chunk_end[c_chunk] - 1
    c_flags = (valid * (1 + 2 * first + 4 * last)).astype(I32)
    return (f_tile.astype(I32), f_chunk.astype(I32), f_flags, te, off.astype(I32),
            c_chunk.astype(I32), c_tile.astype(I32), c_flags, nt_max, w_max)


def moe(x, g, w_router, b_router, wg, wu, wd, tc=MOE_CHUNK, ts=MOE_SLOT_TILE):
    t, d = x.shape
    ne, _, f = wg.shape
    tc = min(tc, t)
    h, idx, gate, cend = moe_route(x, g, w_router, b_router, tc)
    (f_tile, f_chunk, f_flags, te, off, c_chunk, c_tile, c_flags, nt_max, w_max) = _work_lists(
        cend[:, :, 0], t, tc, ts, ne)

    sorted_out = pl.pallas_call(
        functools.partial(_moe_ffn_kernel, ts=ts, ne=ne),
        grid_spec=pltpu.PrefetchScalarGridSpec(
            num_scalar_prefetch=5,
            grid=(w_max,),
            in_specs=[
                pl.BlockSpec((ne, tc), lambda w, wt, wc, wf, te_, of: (0, wc[w])),
                pl.BlockSpec((tc, d), lambda w, wt, wc, wf, te_, of: (wc[w], 0)),
                pl.BlockSpec((None, d, f), lambda w, wt, wc, wf, te_, of: (te_[wt[w]], 0, 0)),
                pl.BlockSpec((None, d, f), lambda w, wt, wc, wf, te_, of: (te_[wt[w]], 0, 0)),
                pl.BlockSpec((None, f, d), lambda w, wt, wc, wf, te_, of: (te_[wt[w]], 0, 0)),
            ],
            out_specs=pl.BlockSpec((ts, d), lambda w, wt, wc, wf, te_, of: (wt[w], 0)),
            scratch_shapes=[pltpu.VMEM((ts, d), F32)],
        ),
        out_shape=jax.ShapeDtypeStruct((nt_max * ts, d), BF16),
        compiler_params=_params("arbitrary"),
        name="moe_ffn",
    )(f_tile, f_chunk, f_flags, te, off, idx, h, wg, wu, wd)

    return pl.pallas_call(
        functools.partial(_moe_combine_kernel, ts=ts, ne=ne),
        grid_spec=pltpu.PrefetchScalarGridSpec(
            num_scalar_prefetch=4,
            grid=(w_max,),
            in_specs=[
                pl.BlockSpec((ne, tc), lambda w, wc, wt, wf, of: (0, wc[w])),
                pl.BlockSpec((ne, tc), lambda w, wc, wt, wf, of: (0, wc[w])),
                pl.BlockSpec((tc, d), lambda w, wc, wt, wf, of: (wc[w], 0)),
                pl.BlockSpec((ts, d), lambda w, wc, wt, wf, of: (wt[w], 0)),
            ],
            out_specs=pl.BlockSpec((tc, d), lambda w, wc, wt, wf, of: (wc[w], 0)),
            scratch_shapes=[pltpu.VMEM((tc, d), F32)],
        ),
        out_shape=jax.ShapeDtypeStruct((t, d), F32),
        compiler_params=_params("arbitrary"),
        name="moe_combine",
    )(c_chunk, c_tile, c_flags, off, idx, gate, x, sorted_out)


def _lambda_init(layer):
    return 0.8 - 0.6 * math.exp(-0.3 * layer)


def kernel(x_prompt, x_sample, state_ssm, state_conv, cache_k, cache_v, page_table, norm_mix_g, norm_ffn_g, norm_kv_g, norm_final_g, ssm_w_in, ssm_conv_w, ssm_conv_b, ssm_dt_bias, ssm_a_log, ssm_d, ssm_norm_g, ssm_w_out, kv_w_k, kv_w_v, att_w_q, att_lam_q1, att_lam_k1, att_lam_q2, att_lam_k2, att_subln_g, att_w_o, ffn_w_gate, ffn_w_up, ffn_w_down, moe_w_router, moe_b_router, moe_w_gate, moe_w_up, moe_w_down):
    depth, d_model = norm_mix_g.shape
    n_a = ssm_w_in.shape[0]
    heads_ssm = ssm_a_log.shape[1]
    d_inner = heads_ssm * SSM_HEAD_DIM
    conv_dim = ssm_conv_w.shape[2]
    att_dv = att_subln_g.shape[1]
    att_heads = kv_w_v.shape[1] // att_dv
    slopes = 2.0 ** (-8.0 * jnp.arange(1, att_heads + 1, dtype=F32) / att_heads)

    in_dim = ssm_w_in.shape[2]
    w_in = jnp.pad(ssm_w_in, ((0, 0), (0, 0), (0, d_inner + conv_dim + LANE - in_dim))).astype(BF16)
    w_out = ssm_w_out.astype(BF16)
    w_k, w_v = kv_w_k.astype(BF16), kv_w_v.astype(BF16)
    w_q = att_w_q.astype(BF16)
    w_o = att_w_o.astype(BF16)
    f_gate, f_up, f_down = ffn_w_gate.astype(BF16), ffn_w_up.astype(BF16), ffn_w_down.astype(BF16)
    m_gate, m_up, m_down = moe_w_gate.astype(BF16), moe_w_up.astype(BF16), moe_w_down.astype(BF16)
    lam_vecs = jnp.stack([att_lam_q1, att_lam_k1, att_lam_q2, att_lam_k2], axis=1)

    def trunk(x3, ssm0, conv0, past):
        b, l, _ = x3.shape
        x = x3.reshape(b * l, d_model)
        states = None
        k_new = v_new = k_att = v_att = None
        for layer in range(depth):
            if layer < n_a:
                ssd_w = (ssm_conv_w[layer], ssm_conv_b[layer], ssm_dt_bias[layer], ssm_a_log[layer],
                         ssm_d[layer], ssm_norm_g[layer])
                if l % SSD_CHUNK == 0:
                    xo, *states = ssd_mixer(x.reshape(b, l, d_model), layer, n_a, conv0, ssm0, states, *ssd_w,
                                            SSD_CHUNK, fuse=(norm_mix_g[layer], w_in[layer], w_out[layer]))
                    x = xo.reshape(b * l, d_model)
                else:
                    zx = linear(x, w_in[layer], g=norm_mix_g[layer]).reshape(b, l, -1)
                    y, *states = ssd_mixer(zx, layer, n_a, conv0, ssm0, states, *ssd_w, l)
                    x = linear(y.reshape(b * l, d_inner), w_out[layer], residual=x)
            else:
                j = layer - n_a
                lam_init = _lambda_init(layer)
                if past is None:
                    qt = linear(x, w_q[j], g=norm_mix_g[layer], scale=ATT_HEAD_DIM ** -0.5, outs=("bf16_t",),
                                tm=ATT_TILE)
                    o = attention_prompt(qt.reshape(b, l // ATT_TILE, -1, ATT_TILE), k_att, v_att,
                                         lam_vecs[j], att_subln_g[j], slopes, lam_init)
                else:
                    q = linear(x, w_q[j], g=norm_mix_g[layer])
                    o = attention_decode(q.reshape(b, l, -1), k_att, v_att, past[0], past[1], page_table,
                                         lam_vecs[j], att_subln_g[j], slopes, lam_init)
                x = linear(o.reshape(b * l, -1), w_o[j], residual=x)
            i = layer // 2
            if layer % 2 == 0:
                x = ffn(x, norm_ffn_g[layer], f_gate[i], f_up[i], f_down[i])
            else:
                x = moe(x, norm_ffn_g[layer], moe_w_router[i], moe_b_router[i], m_gate[i], m_up[i], m_down[i])
            if layer == n_a - 1:
                if past is None:
                    assert l % ATT_TILE == 0
                    k_new, k_att = linear(x, w_k, g=norm_kv_g, outs=("f32", "bf16"))
                    v_new, v_att = linear(x, w_v, g=norm_kv_g, outs=("f32", "bf16_t"), tm=ATT_TILE)
                    k_att = k_att.reshape(b, l, -1)
                    v_att = v_att.reshape(b, l // ATT_TILE, -1, ATT_TILE)
                else:
                    k_att = k_new = linear(x, w_k, g=norm_kv_g).reshape(b, l, -1)
                    v_att = v_new = linear(x, w_v, g=norm_kv_g).reshape(b, l, -1)
        y = rmsnorm(x, norm_final_g).reshape(b, l, d_model)
        return (y, states[0].reshape(n_a, b, heads_ssm, SSM_HEAD_DIM, SSM_STATE), states[1],
                k_new.reshape(b, l, att_heads, att_dv), v_new.reshape(b, l, att_heads, att_dv))

    y_p, ssm_p, conv_p, k_p, v_p = trunk(x_prompt, None, None, None)
    bs = x_sample.shape[0]
    y_s, ssm_s, conv_s, k_s, v_s = trunk(x_sample, state_ssm.reshape(n_a, bs, d_inner, SSM_STATE), state_conv,
                                         (cache_k, cache_v))
    return (y_p, y_s, ssm_p, conv_p, k_p, v_p, ssm_s, conv_s, k_s, v_s)
```

```python
import functools
import math

import jax
import jax.numpy as jnp
from jax import lax
from jax.experimental import pallas as pl
from jax.experimental.pallas import tpu as pltpu

F32 = jnp.float32
BF16 = jnp.bfloat16
I32 = jnp.int32

RMS_EPS = 1e-5
SSM_HEAD_DIM = 64
SSM_GROUPS = 8
SSM_STATE = 128
CONV_WIDTH = 4
SSD_CHUNK = 128
ATT_HEAD_DIM = 64
N_EXPERTS = 8
LANE = 128
SUBLANE = 8
VMEM_LIMIT = 56 * 1024 * 1024
NEG = -1e30

TOKEN_TILE = 256
MOE_SLOT_TILE = 256
MOE_CHUNK = 1024
ATT_TILE = 256
PAGES_PER_STEP = 16
DECODE_GROUP = 4


def _params(*sem):
    return pltpu.CompilerParams(dimension_semantics=sem, vmem_limit_bytes=VMEM_LIMIT)


def _resident(shape):
    nd = len(shape)
    return pl.BlockSpec(shape, lambda *_: (0,) * nd, pipeline_mode=pl.Buffered(1))


def _rms(x, g):
    ms = jnp.mean(x * x, axis=-1, keepdims=True)
    return x * lax.rsqrt(ms + RMS_EPS) * g


def _silu(x):
    return (0.5 * x) * (1.0 + jnp.tanh(0.5 * x))


def _dot(a, b):
    return jnp.dot(a, b, preferred_element_type=F32)


def _dot_nt(a, b):
    return lax.dot_general(a, b, (((1,), (1,)), ((), ())), preferred_element_type=F32)


def _dot_tn(a, b):
    return lax.dot_general(a, b, (((0,), (0,)), ((), ())), preferred_element_type=F32)


def _split2(x):
    hi = x.astype(BF16)
    lo = (x - hi.astype(F32)).astype(BF16)
    return hi, lo


def _split3(x):
    hi = x.astype(BF16)
    r = x - hi.astype(F32)
    mid = r.astype(BF16)
    lo = (r - mid.astype(F32)).astype(BF16)
    return hi, mid, lo


def _linear_kernel(*refs, norm, residual, scale, outs):
    it = iter(refs)
    x_ref = next(it)
    g_ref = next(it) if norm else None
    w_ref = next(it)
    r_ref = next(it) if residual else None
    x = x_ref[...].astype(F32)
    if norm:
        x = _rms(x, g_ref[...])
    y = _dot(x.astype(BF16), w_ref[...])
    if residual:
        y = y + r_ref[...]
    if scale is not None:
        y = y * scale
    for kind in outs:
        o_ref = next(it)
        if kind == "f32":
            o_ref[...] = y
        elif kind == "bf16":
            o_ref[...] = y.astype(BF16)
        else:
            o_ref[...] = jnp.transpose(y).astype(BF16)


def linear(x, w, g=None, residual=None, scale=None, outs=("f32",), tm=TOKEN_TILE):
    t, k = x.shape
    n = w.shape[1]
    tm = min(tm, t)
    assert t % tm == 0
    args = [x]
    specs = [pl.BlockSpec((tm, k), lambda i: (i, 0))]
    if g is not None:
        args.append(g.reshape(1, k))
        specs.append(_resident((1, k)))
    args.append(w)
    specs.append(_resident((k, n)))
    if residual is not None:
        args.append(residual)
        specs.append(pl.BlockSpec((tm, n), lambda i: (i, 0)))
    out_specs, out_shapes = [], []
    for kind in outs:
        if kind == "bf16_t":
            out_specs.append(pl.BlockSpec((None, n, tm), lambda i: (i, 0, 0)))
            out_shapes.append(jax.ShapeDtypeStruct((t // tm, n, tm), BF16))
        else:
            out_specs.append(pl.BlockSpec((tm, n), lambda i: (i, 0)))
            out_shapes.append(jax.ShapeDtypeStruct((t, n), F32 if kind == "f32" else BF16))
    res = pl.pallas_call(
        functools.partial(_linear_kernel, norm=g is not None, residual=residual is not None,
                          scale=scale, outs=tuple(outs)),
        grid=(t // tm,),
        in_specs=specs,
        out_specs=out_specs,
        out_shape=out_shapes,
        compiler_params=_params("parallel"),
        name="linear",
    )(*args)
    return res[0] if len(outs) == 1 else res


def _ffn_kernel(x_ref, g_ref, wg_ref, wu_ref, wd_ref, o_ref):
    x = x_ref[...]
    h = _rms(x, g_ref[...]).astype(BF16)
    a = _dot(h, wg_ref[...])
    u = _dot(h, wu_ref[...])
    act = (_silu(a) * u).astype(BF16)
    o_ref[...] = x + _dot(act, wd_ref[...])


def ffn(x, g, wg, wu, wd, tm=TOKEN_TILE):
    t, d = x.shape
    f = wg.shape[1]
    tm = min(tm, t)
    assert t % tm == 0
    return pl.pallas_call(
        _ffn_kernel,
        grid=(t // tm,),
        in_specs=[pl.BlockSpec((tm, d), lambda i: (i, 0)), _resident((1, d)),
                  _resident((d, f)), _resident((d, f)), _resident((f, d))],
        out_specs=pl.BlockSpec((tm, d), lambda i: (i, 0)),
        out_shape=jax.ShapeDtypeStruct((t, d), F32),
        compiler_params=_params("parallel"),
        name="ffn",
    )(x, g.reshape(1, d), wg, wu, wd)


def _rmsnorm_kernel(x_ref, g_ref, o_ref):
    o_ref[...] = _rms(x_ref[...], g_ref[...])


def rmsnorm(x, g, tm=TOKEN_TILE):
    t, d = x.shape
    tm = min(tm, t)
    assert t % tm == 0
    return pl.pallas_call(
        _rmsnorm_kernel,
        grid=(t // tm,),
        in_specs=[pl.BlockSpec((tm, d), lambda i: (i, 0)), _resident((1, d))],
        out_specs=pl.BlockSpec((tm, d), lambda i: (i, 0)),
        out_shape=jax.ShapeDtypeStruct((t, d), F32),
        compiler_params=_params("parallel"),
        name="rmsnorm",
    )(x, g.reshape(1, d))


def _ssd_kernel(*refs, q, rows, valid, d_inner, heads, fused, has_init, has_prev):
    it = iter(refs)
    src_ref = next(it)
    if fused:
        gmix_ref, win_ref, wout_ref = next(it), next(it), next(it)
    if has_init:
        conv0_ref, ssm0_ref = next(it), next(it)
    cw_ref, cb_ref, dtb_ref, alog_ref, dsk_ref, ng_ref, exp_ref = (next(it) for _ in range(7))
    if has_prev:
        next(it), next(it)
    out_ref, sfin_ref, cnew_ref, xp_ref, stage_ref = (next(it) for _ in range(5))
    ybuf_ref = next(it) if fused else None

    c = pl.program_id(1)
    nc = pl.num_programs(1)
    kp = SSD_CHUNK
    gn = SSM_GROUPS * SSM_STATE
    hpg = heads // SSM_GROUPS
    gw = hpg * SSM_HEAD_DIM
    halo = CONV_WIDTH - 1

    @pl.when(c == 0)
    def _():
        xp_ref[0:SUBLANE, :] = jnp.zeros((SUBLANE, xp_ref.shape[1]), F32)
        if has_init:
            sfin_ref[...] = ssm0_ref[...]
            xp_ref[SUBLANE - halo:SUBLANE, :] = conv0_ref[...]
        else:
            sfin_ref[...] = jnp.zeros(sfin_ref.shape, F32)

    if rows < q:
        stage_ref[...] = jnp.zeros(stage_ref.shape, F32)
        stage_ref[0:rows, :] = src_ref[...]
        src = stage_ref[...]
    else:
        src = src_ref[...]
    if fused:
        zx = _dot(_rms(src, gmix_ref[...]).astype(BF16), win_ref[...])
    else:
        zx = src
    z = zx[:, :d_inner]
    xbc = zx[:, d_inner:2 * d_inner + 2 * gn]
    dtr = zx[:, 2 * d_inner + 2 * gn:]

    xp_ref[SUBLANE:SUBLANE + q, :] = xbc
    xall = xp_ref[...]
    conv = cb_ref[...] + xall[SUBLANE:] * cw_ref[halo:halo + 1, :]
    for shift in range(1, CONV_WIDTH):
        k = halo - shift
        conv = conv + pltpu.roll(xall, shift, 0)[SUBLANE:] * cw_ref[k:k + 1, :]
    tail = xp_ref[SUBLANE + valid - halo:SUBLANE + valid, :]
    xp_ref[SUBLANE - halo:SUBLANE, :] = tail

    @pl.when(c == nc - 1)
    def _():
        cnew_ref[...] = tail

    xa = _silu(conv)
    xs = xa[:, :d_inner]
    bm = xa[:, d_inner:d_inner + gn]
    cm = xa[:, d_inner + gn:]

    def pad(v):
        if q == kp:
            return v
        return jnp.concatenate([v, jnp.zeros((kp - q, v.shape[1]), v.dtype)], axis=0)

    row = lax.broadcasted_iota(I32, (q, kp), 0)
    col = lax.broadcasted_iota(I32, (q, kp), 1)
    causal = col <= row

    dtv = dtr + dtb_ref[...]
    dt = jnp.maximum(dtv, 0.0) + jnp.log1p(jnp.exp(-jnp.abs(dtv)))
    if valid < q:
        dt = jnp.where(lax.broadcasted_iota(I32, (q, LANE), 0) < valid, dt, 0.0)
    adt = dt * (-jnp.exp(alog_ref[...]))
    tril = causal.astype(BF16)
    cs = sum(_dot(tril, part) for part in _split3(pad(adt)))
    cs_t = jnp.transpose(pad(cs))
    cs_last = cs[q - 1:q, :]
    e_last = jnp.exp(cs_last)

    stacked = jnp.concatenate([dt, jnp.exp(cs), jnp.exp(cs_last - cs)], axis=0)
    hi, lo = _split2(stacked)
    wide = _dot(hi, exp_ref[...]) + _dot(lo, exp_ref[...])
    dt_w, ecs_w, dout_w = wide[0:q], wide[q:2 * q], wide[2 * q:3 * q]
    xdt = xs * dt_w
    xdtd = pad(xdt * dout_w).astype(BF16)
    xdt_b = pad(xdt).astype(BF16)
    bm_p = pad(bm).astype(BF16)
    cm_b = cm.astype(BF16)
    lane_g = lax.broadcasted_iota(I32, (kp, gw), 1) // SSM_HEAD_DIM
    head_mask = [(lane_g == r).astype(BF16) for r in range(hpg)]

    for g in range(SSM_GROUPS):
        b_g = bm_p[:, g * SSM_STATE:(g + 1) * SSM_STATE]
        c_g = cm_b[:, g * SSM_STATE:(g + 1) * SSM_STATE]
        cb = _dot_nt(c_g, b_g)
        xg = xdt_b[:, g * gw:(g + 1) * gw]
        m_parts, x_parts, scale_parts = [], [], []
        for r in range(hpg):
            h = g * hpg + r
            seg = cs[:, h:h + 1] - cs_t[h:h + 1, :]
            decay = jnp.exp(jnp.where(causal, seg, NEG))
            m_parts.append((cb * decay).astype(BF16))
            x_parts.append(xg * head_mask[r])
            scale_parts.append(jnp.broadcast_to(e_last[:, h:h + 1], (SSM_HEAD_DIM, SSM_STATE)))
        y_diag = _dot(jnp.concatenate(m_parts, axis=1), jnp.concatenate(x_parts, axis=0))
        s_g = sfin_ref[g * gw:(g + 1) * gw, :]
        y_off = _dot_nt(c_g, s_g.astype(BF16)) * ecs_w[:, g * gw:(g + 1) * gw]
        new = _dot_tn(xdtd[:, g * gw:(g + 1) * gw], b_g)
        sfin_ref[g * gw:(g + 1) * gw, :] = s_g * jnp.concatenate(scale_parts, axis=0) + new
        y = y_diag + y_off + dsk_ref[:, g * gw:(g + 1) * gw] * xs[:, g * gw:(g + 1) * gw]
        y = y * _silu(z[:, g * gw:(g + 1) * gw])
        y = y * lax.rsqrt(jnp.mean(y * y, axis=-1, keepdims=True) + RMS_EPS)
        y = y * ng_ref[:, g * gw:(g + 1) * gw]
        if fused:
            ybuf_ref[:, g * gw:(g + 1) * gw] = y.astype(BF16)
        else:
            out_ref[:, g * gw:(g + 1) * gw] = y

    if fused:
        out_ref[...] = src + _dot(ybuf_ref[...], wout_ref[...])


def ssd_mixer(src, layer, n_layers, conv0_all, ssm0_all, prev, conv_w, conv_b, dt_bias, a_log, d_skip,
              norm_g, valid, fuse=None):
    b, l, width = src.shape
    heads = a_log.shape[0]
    d_inner = heads * SSM_HEAD_DIM
    conv_dim = conv_w.shape[1]
    if l % SSD_CHUNK == 0:
        q = rows = SSD_CHUNK
        nc = l // SSD_CHUNK
        assert valid == SSD_CHUNK
    else:
        assert l < SSD_CHUNK
        rows, nc = l, 1
        q = -(-l // SUBLANE) * SUBLANE
    padh = LANE - heads
    expand = jnp.repeat(jnp.eye(LANE, heads, dtype=BF16), SSM_HEAD_DIM, axis=1)
    halo = CONV_WIDTH - 1

    args = [src]
    specs = [pl.BlockSpec((None, rows, width), lambda i, j: (i, j, 0))]
    if fuse is not None:
        g_mix, w_in, w_out = fuse
        assert w_in.shape == (width, d_inner + conv_dim + LANE)
        args += [g_mix.reshape(1, width), w_in, w_out]
        specs += [_resident((1, width)), _resident(w_in.shape), _resident(w_out.shape)]
        out_width = w_out.shape[1]
    else:
        assert width == d_inner + conv_dim + LANE
        out_width = d_inner
    if ssm0_all is not None:
        args += [conv0_all, ssm0_all]
        specs += [pl.BlockSpec((None, None, halo, conv_dim), lambda i, j: (layer, i, 0, 0)),
                  pl.BlockSpec((None, None, d_inner, SSM_STATE), lambda i, j: (layer, i, 0, 0))]
    args += [conv_w, conv_b.reshape(1, conv_dim),
             jnp.pad(dt_bias, (0, padh)).reshape(1, LANE), jnp.pad(a_log, (0, padh)).reshape(1, LANE),
             jnp.repeat(d_skip, SSM_HEAD_DIM).reshape(1, d_inner), norm_g.reshape(1, d_inner), expand]
    specs += [_resident((CONV_WIDTH, conv_dim)), _resident((1, conv_dim)), _resident((1, LANE)),
              _resident((1, LANE)), _resident((1, d_inner)), _resident((1, d_inner)),
              _resident((LANE, d_inner))]
    aliases = {}
    if prev is not None:
        aliases = {len(args): 1, len(args) + 1: 2}
        args += list(prev)
        specs += [pl.BlockSpec(memory_space=pl.ANY), pl.BlockSpec(memory_space=pl.ANY)]
    scratch = [pltpu.VMEM((SUBLANE + q, conv_dim), F32),
               pltpu.VMEM((q, width) if rows < q else (SUBLANE, LANE), F32)]
    if fuse is not None:
        scratch.append(pltpu.VMEM((q, d_inner), BF16))
    kern = functools.partial(_ssd_kernel, q=q, rows=rows, valid=valid, d_inner=d_inner, heads=heads,
                             fused=fuse is not None, has_init=ssm0_all is not None, has_prev=prev is not None)
    out, sfin, cnew = pl.pallas_call(
        kern,
        grid=(b, nc),
        in_specs=specs,
        out_specs=[
            pl.BlockSpec((None, q, out_width), lambda i, j: (i, j, 0)),
            pl.BlockSpec((None, None, d_inner, SSM_STATE), lambda i, j: (layer, i, 0, 0)),
            pl.BlockSpec((None, None, halo, conv_dim), lambda i, j: (layer, i, 0, 0)),
        ],
        out_shape=[
            jax.ShapeDtypeStruct((b, nc * q, out_width), F32),
            jax.ShapeDtypeStruct((n_layers, b, d_inner, SSM_STATE), F32),
            jax.ShapeDtypeStruct((n_layers, b, halo, conv_dim), F32),
        ],
        scratch_shapes=scratch,
        input_output_aliases=aliases,
        compiler_params=_params("parallel", "arbitrary"),
        name="ssd_mixer",
    )(*args)
    return out[:, :l], sfin, cnew


def _lambda(lam_ref, lam_init):
    lv = lam_ref[...]
    s1 = jnp.sum(lv[0:1] * lv[1:2], axis=-1, keepdims=True)
    s2 = jnp.sum(lv[2:3] * lv[3:4], axis=-1, keepdims=True)
    return jnp.exp(s1) - jnp.exp(s2) + lam_init


def _two_branch_q(qv, n):
    lane = lax.broadcasted_iota(I32, qv.shape, 1)
    zero = jnp.zeros_like(qv)
    return jnp.concatenate([jnp.where(lane < ATT_HEAD_DIM, qv, zero),
                            jnp.where(lane >= ATT_HEAD_DIM, qv, zero)], axis=0).astype(BF16)


def _attn_prompt_kernel(slope_ref, qt_ref, k_ref, vt_ref, lam_ref, sg_ref, o_ref, *, tq, lam_init):
    slope = slope_ref[pl.program_id(1)]
    nq, dv, _ = qt_ref.shape
    feat = lax.broadcasted_iota(I32, (dv, tq), 0)
    cc = lax.broadcasted_iota(I32, (tq, 2 * tq), 1)
    rel = jnp.where(cc >= tq, cc - tq, cc) - lax.broadcasted_iota(I32, (tq, 2 * tq), 0)
    bias = -slope * rel.astype(F32)
    bias_diag = jnp.where(rel >= 0, bias, NEG)
    lam = _lambda(lam_ref, lam_init)
    m0 = jnp.full((1, 2 * tq), NEG, F32)
    l0 = jnp.zeros((1, 2 * tq), F32)
    a0 = jnp.zeros((dv, 2 * tq), F32)

    def q_tile(qi, _):
        qt = qt_ref[qi]
        zero = jnp.zeros_like(qt)
        q2t = jnp.concatenate([jnp.where(feat < ATT_HEAD_DIM, qt, zero),
                               jnp.where(feat >= ATT_HEAD_DIM, qt, zero)], axis=1)

        def scores(j):
            start = pl.multiple_of(j * tq, tq)
            return _dot(k_ref[pl.ds(start, tq), :], q2t)

        def consume(j, s, carry):
            m, l, acc = carry
            tile_bias = -slope * ((qi - j) * tq).astype(F32)
            m_new = jnp.maximum(m, jnp.max(s, axis=0, keepdims=True) + tile_bias)
            alpha = jnp.exp(m - m_new)
            p = jnp.exp(s - (m_new - tile_bias))
            l = alpha * l + jnp.sum(p, axis=0, keepdims=True)
            acc = alpha * acc + _dot(vt_ref[j], p.astype(BF16))
            return m_new, l, acc

        def body(j, c):
            s_next = scores(j + 1)
            return consume(j, c[3] + bias, c[:3]) + (s_next,)

        c = lax.fori_loop(0, qi, body, (m0, l0, a0, scores(0)))
        _, l, acc = consume(qi, c[3] + bias_diag, c[:3])
        o = acc / l
        o = jnp.transpose(o[:, :tq] - lam * o[:, tq:])
        o_ref[pl.ds(pl.multiple_of(qi * tq, tq), tq), :] = _rms(o, sg_ref[...]) * (1.0 - lam_init)
        return 0

    lax.fori_loop(0, nq, q_tile, 0)


def attention_prompt(qt, k, vt, lam_vecs, subln_g, slopes, lam_init):
    b, nq, hd, tq = qt.shape
    l = nq * tq
    dv = subln_g.shape[0]
    heads = hd // dv
    return pl.pallas_call(
        functools.partial(_attn_prompt_kernel, tq=tq, lam_init=lam_init),
        grid_spec=pltpu.PrefetchScalarGridSpec(
            num_scalar_prefetch=1,
            grid=(b, heads),
            in_specs=[
                pl.BlockSpec((None, nq, dv, tq), lambda i, h, s: (i, 0, h, 0)),
                pl.BlockSpec((None, l, dv), lambda i, h, s: (i, 0, h)),
                pl.BlockSpec((None, nq, dv, tq), lambda i, h, s: (i, 0, h, 0)),
                pl.BlockSpec(lam_vecs.shape, lambda i, h, s: (0, 0)),
                pl.BlockSpec((1, dv), lambda i, h, s: (0, 0)),
            ],
            out_specs=pl.BlockSpec((None, l, dv), lambda i, h, s: (i, 0, h)),
        ),
        out_shape=jax.ShapeDtypeStruct((b, l, hd), F32),
        compiler_params=_params("parallel", "parallel"),
        name="attention_prompt",
    )(slopes, qt, k, vt, lam_vecs, subln_g.reshape(1, dv))


def _attn_decode_kernel(pt_ref, slope_ref, q_ref, kn_ref, vn_ref, lam_ref, sg_ref, table_ref, *rest,
                        pp, group, heads, page, n_new, lam_init):
    k_refs = rest[:pp]
    v_refs = rest[pp:2 * pp]
    o_ref = rest[2 * pp]
    m_ref, l_ref, acc_ref = rest[2 * pp + 1:]
    j = pl.program_id(1)
    nj = pl.num_programs(1)
    dv = sg_ref.shape[1]
    nr = 2 * SUBLANE
    pairs = page * heads
    slope_col = jnp.concatenate([jnp.full((nr, 1), slope_ref[h], F32) for h in range(heads)], axis=0)

    @pl.when(j == 0)
    def _():
        m_ref[...] = jnp.full(m_ref.shape, NEG, F32)
        l_ref[...] = jnp.zeros(l_ref.shape, F32)
        acc_ref[...] = jnp.zeros(acc_ref.shape, F32)

    qv = q_ref[...] * (ATT_HEAD_DIM ** -0.5)
    q2 = [_two_branch_q(qv[:, h * dv:(h + 1) * dv], SUBLANE) for h in range(heads)]
    q_all = jnp.concatenate(q2, axis=0)

    def update(s, shift, pv_of):
        m = m_ref[...]
        m_new = jnp.maximum(m, jnp.max(s, axis=-1, keepdims=True) + shift)
        alpha = jnp.exp(m - m_new)
        pr = jnp.exp(s - (m_new - shift))
        l_ref[...] = alpha * l_ref[...] + jnp.sum(pr, axis=-1, keepdims=True)
        acc_ref[...] = alpha * acc_ref[...] + pv_of(pr.astype(BF16))
        m_ref[...] = m_new

    for g0 in range(0, pp, group):
        ks = [k_refs[p][...].astype(BF16) for p in range(g0, g0 + group)]
        vs = [v_refs[p][...].astype(BF16) for p in range(g0, g0 + group)]
        s = jnp.concatenate([_dot_nt(q_all, kb) for kb in ks], axis=1) + table_ref[...]
        start = ((j * pp + g0) * page).astype(F32)
        update(s, slope_col * start,
               lambda pb: sum(_dot(pb[:, i * pairs:(i + 1) * pairs], vs[i]) for i in range(group)))

    @pl.when(j == nj - 1)
    def _():
        row = lax.broadcasted_iota(I32, (heads * nr, page), 0)
        col = lax.broadcasted_iota(I32, (heads * nr, page), 1)
        zeros = jnp.zeros((page - SUBLANE, kn_ref.shape[1]), F32)
        kb = jnp.concatenate([kn_ref[...], zeros], axis=0).astype(BF16)
        vb = jnp.concatenate([vn_ref[...], zeros], axis=0).astype(BF16)
        dist = (row % SUBLANE) - col
        ok = (dist >= 0) & (col < n_new)
        s = jnp.concatenate([_dot_nt(q2[h], kb[:, h * dv:(h + 1) * dv]) for h in range(heads)], axis=0)
        update(s + jnp.where(ok, -slope_col * dist.astype(F32), NEG), jnp.zeros_like(slope_col),
               lambda pb: jnp.concatenate([_dot(pb[h * nr:(h + 1) * nr], vb[:, h * dv:(h + 1) * dv])
                                           for h in range(heads)], axis=0))
        o = acc_ref[...] / l_ref[...]
        lam = _lambda(lam_ref, lam_init)
        for h in range(heads):
            oh = o[h * nr:h * nr + SUBLANE] - lam * o[h * nr + SUBLANE:(h + 1) * nr]
            o_ref[:, h * dv:(h + 1) * dv] = _rms(oh, sg_ref[...]) * (1.0 - lam_init)


def attention_decode(q, k_new, v_new, cache_k, cache_v, page_table, lam_vecs, subln_g, slopes, lam_init):
    b, n_new, hd = q.shape
    dv = subln_g.shape[0]
    heads = hd // dv
    n_pages = page_table.shape[1]
    page = cache_k.shape[1]
    pp = math.gcd(PAGES_PER_STEP, n_pages)
    assert n_new <= SUBLANE
    padr = ((0, 0), (0, SUBLANE - n_new), (0, 0))
    qp, kp, vp = (jnp.pad(a, padr) for a in (q, k_new, v_new))

    n_pool = cache_k.shape[0]
    ck, cv = (c.reshape(n_pool, page * heads, dv) for c in (cache_k, cache_v))

    group = math.gcd(DECODE_GROUP, pp)
    nrow = heads * 2 * SUBLANE
    r = jnp.arange(nrow, dtype=I32)[:, None]
    c = jnp.arange(group * page * heads, dtype=I32)[None, :]
    qpos = (n_pages * page + r % SUBLANE).astype(F32)
    table = jnp.where(r // (2 * SUBLANE) == c % heads,
                      -slopes[r // (2 * SUBLANE)] * (qpos - (c // heads).astype(F32)), NEG)

    def page_spec(p_i):
        return pl.BlockSpec((None, page * heads, dv),
                            lambda i, j, pt, s: (pt[i * n_pages + j * pp + p_i], 0, 0))

    row_spec = pl.BlockSpec((None, SUBLANE, hd), lambda i, j, pt, s: (i, 0, 0))
    out = pl.pallas_call(
        functools.partial(_attn_decode_kernel, pp=pp, group=group, heads=heads, page=page, n_new=n_new,
                          lam_init=lam_init),
        grid_spec=pltpu.PrefetchScalarGridSpec(
            num_scalar_prefetch=2,
            grid=(b, n_pages // pp),
            in_specs=[row_spec, row_spec, row_spec,
                      pl.BlockSpec(lam_vecs.shape, lambda i, j, pt, s: (0, 0)),
                      pl.BlockSpec((1, dv), lambda i, j, pt, s: (0, 0)),
                      pl.BlockSpec(table.shape, lambda i, j, pt, s: (0, 0), pipeline_mode=pl.Buffered(1))]
                     + [page_spec(p_i) for p_i in range(pp)] * 2,
            out_specs=row_spec,
            scratch_shapes=[pltpu.VMEM((heads * 2 * SUBLANE, 1), F32),
                            pltpu.VMEM((heads * 2 * SUBLANE, 1), F32),
                            pltpu.VMEM((heads * 2 * SUBLANE, dv), F32)],
        ),
        out_shape=jax.ShapeDtypeStruct((b, SUBLANE, hd), F32),
        compiler_params=_params("parallel", "arbitrary"),
        name="attention_decode",
    )(page_table.reshape(-1), slopes, qp, kp, vp, lam_vecs, subln_g.reshape(1, dv), table,
      *([ck] * pp), *([cv] * pp))
    return out[:, :n_new]


def _router_kernel(x_ref, g_ref, wr_ref, br_ref, h_ref, idx_ref, gate_ref, cend_ref, run_ref):
    i = pl.program_id(0)
    tm = x_ref.shape[0]

    @pl.when(i == 0)
    def _():
        run_ref[...] = jnp.zeros(run_ref.shape, F32)

    h = _rms(x_ref[...], g_ref[...])
    h_ref[...] = h.astype(BF16)
    logits = lax.dot_general(wr_ref[...], h, (((1,), (1,)), ((), ())), precision=lax.Precision.HIGHEST,
                             preferred_element_type=F32) + br_ref[...]
    ne = logits.shape[0]
    eid = lax.broadcasted_iota(I32, (ne, tm), 0)
    eidf = eid.astype(F32)
    v0 = jnp.max(logits, axis=0, keepdims=True)
    i0 = jnp.min(jnp.where(logits == v0, eidf, float(ne)), axis=0, keepdims=True).astype(I32)
    rest = jnp.where(eid == i0, -jnp.inf, logits)
    v1 = jnp.max(rest, axis=0, keepdims=True)
    i1 = jnp.min(jnp.where(rest == v1, eidf, float(ne)), axis=0, keepdims=True).astype(I32)
    e = jnp.exp(v1 - v0)
    g0 = 1.0 / (1.0 + e)
    g1 = e / (1.0 + e)
    sel0 = eid == i0
    sel1 = eid == i1
    assign = (sel0 | sel1).astype(BF16)
    before = (lax.broadcasted_iota(I32, (tm, tm), 0) < lax.broadcasted_iota(I32, (tm, tm), 1)).astype(BF16)
    rank = _dot(assign, before) + run_ref[...]
    r0 = jnp.sum(jnp.where(sel0, rank, 0.0), axis=0, keepdims=True).astype(I32)
    r1 = jnp.sum(jnp.where(sel1, rank, 0.0), axis=0, keepdims=True).astype(I32)
    idx_ref[...] = jnp.where(eid == 0, i0, jnp.where(eid == 1, i1, jnp.where(eid == 2, r0,
                             jnp.where(eid == 3, r1, 0))))
    gate_ref[...] = jnp.where(eid == 0, g0, jnp.where(eid == 1, g1, 0.0))
    run = run_ref[...] + jnp.sum(assign.astype(F32), axis=1, keepdims=True)
    run_ref[...] = run
    cend_ref[...] = jnp.broadcast_to(run, cend_ref.shape)


def moe_route(x, g, w_router, b_router, tc=MOE_CHUNK):
    t, d = x.shape
    ne = w_router.shape[1]
    assert t % tc == 0 and ne == SUBLANE
    nchunk = t // tc
    return pl.pallas_call(
        _router_kernel,
        grid=(nchunk,),
        in_specs=[pl.BlockSpec((tc, d), lambda i: (i, 0)), _resident((1, d)),
                  _resident((ne, d)), _resident((ne, 1))],
        out_specs=[pl.BlockSpec((tc, d), lambda i: (i, 0)),
                   pl.BlockSpec((ne, tc), lambda i: (0, i)),
                   pl.BlockSpec((ne, tc), lambda i: (0, i)),
                   pl.BlockSpec((None, ne, LANE), lambda i: (i, 0, 0))],
        out_shape=[jax.ShapeDtypeStruct((t, d), BF16), jax.ShapeDtypeStruct((ne, t), I32),
                   jax.ShapeDtypeStruct((ne, t), F32), jax.ShapeDtypeStruct((nchunk, ne, LANE), F32)],
        scratch_shapes=[pltpu.VMEM((ne, 1), F32)],
        compiler_params=_params("arbitrary"),
        name="moe_route",
    )(x, g.reshape(1, d), w_router.T, b_router.reshape(ne, 1))


def _slot_rows(idx_ref, off_ref, slot0, ne):
    idx = idx_ref[...]
    e0, e1, r0, r1 = idx[0:1], idx[1:2], idx[2:3], idx[3:4]
    o0 = jnp.zeros_like(e0)
    o1 = jnp.zeros_like(e1)
    for k in range(ne):
        o0 = jnp.where(e0 == k, off_ref[k], o0)
        o1 = jnp.where(e1 == k, off_ref[k], o1)
    return o0 + r0 - slot0, o1 + r1 - slot0


def _moe_ffn_kernel(wt_ref, wc_ref, wf_ref, te_ref, off_ref, idx_ref, h_ref, wg_ref, wu_ref, wd_ref,
                    o_ref, acc_ref, *, ts, ne):
    w = pl.program_id(0)
    flags = wf_ref[w]

    @pl.when((flags & 2) != 0)
    def _():
        acc_ref[...] = jnp.zeros(acc_ref.shape, F32)

    @pl.when((flags & 1) != 0)
    def _():
        sa, sb = _slot_rows(idx_ref, off_ref, wt_ref[w] * ts, ne)
        rows = lax.broadcasted_iota(I32, (ts, idx_ref.shape[1]), 0)
        pick = ((rows == sa) | (rows == sb)).astype(BF16)
        acc_ref[...] += _dot(pick, h_ref[...])

    @pl.when((flags & 4) != 0)
    def _():
        xb = acc_ref[...].astype(BF16)
        a = _dot(xb, wg_ref[...])
        u = _dot(xb, wu_ref[...])
        o_ref[...] = _dot((_silu(a) * u).astype(BF16), wd_ref[...]).astype(BF16)


def _moe_combine_kernel(wc_ref, wt_ref, wf_ref, off_ref, idx_ref, gate_ref, x_ref, es_ref, o_ref, acc_ref,
                        *, ts, ne):
    w = pl.program_id(0)
    flags = wf_ref[w]

    @pl.when((flags & 2) != 0)
    def _():
        acc_ref[...] = x_ref[...]

    @pl.when((flags & 1) != 0)
    def _():
        sa, sb = _slot_rows(idx_ref, off_ref, wt_ref[w] * ts, ne)
        gt = gate_ref[...]
        rows = lax.broadcasted_iota(I32, (ts, idx_ref.shape[1]), 0)
        wgt = (jnp.where(rows == sa, gt[0:1], 0.0) + jnp.where(rows == sb, gt[1:2], 0.0)).astype(BF16)
        acc_ref[...] += _dot_tn(wgt, es_ref[...])

    @pl.when((flags & 4) != 0)
    def _():
        o_ref[...] = acc_ref[...]


def _work_lists(cend, t, tc, ts, ne):
    nchunk = t // tc
    nt_max = (2 * t) // ts + ne
    w_max = nt_max + ne * nchunk
    cend = cend.astype(I32)
    cstart = jnp.concatenate([jnp.zeros((1, ne), I32), cend[:-1]], axis=0)
    cnt = cend[-1]
    tiles = (cnt + ts - 1) // ts
    tile_end = jnp.cumsum(tiles)
    off = (tile_end - tiles) * ts
    n_tiles = tile_end[-1]
    tile_ids = jnp.arange(nt_max, dtype=I32)

    def count_le(sorted_vals, queries):
        return jnp.sum(sorted_vals[None, :] <= queries[:, None], axis=1).astype(I32)

    te = jnp.minimum(count_le(tile_end, tile_ids), ne - 1)

    def flatten(counts, w_total):
        ends = jnp.cumsum(counts)
        total = ends[-1]
        wi = jnp.clip(jnp.arange(w_total, dtype=I32), 0, jnp.maximum(total - 1, 0))
        owner = jnp.minimum(count_le(ends, wi), counts.shape[0] - 1)
        local = wi - (ends[owner] - counts[owner])
        valid = jnp.arange(w_total, dtype=I32) < total
        return owner, local, valid

    k0 = tile_ids * ts - off[te]
    k1 = jnp.minimum(k0 + ts, cnt[te])
    ce_t = cend[:, te]
    cs_t = cstart[:, te]
    c_lo = jnp.sum(ce_t <= k0[None, :], axis=0).astype(I32)
    c_hi = (nchunk - 1 - jnp.sum(cs_t >= k1[None, :], axis=0)).astype(I32)
    active = tile_ids < n_tiles
    n_items = jnp.where(active, c_hi - c_lo + 1, 0)
    owner, local, valid = flatten(n_items, w_max)
    f_tile, f_chunk = owner, jnp.clip(c_lo[owner] + local, 0, nchunk - 1)
    first = local == 0
    last = local == n_items[owner] - 1
    f_flags = (valid * (1 + 2 * first + 4 * last)).astype(I32)

    has = cend > cstart
    s_lo = (off[None, :] + cstart) // ts
    s_hi = (off[None, :] + cend - 1) // ts
    n_ce = jnp.where(has, s_hi - s_lo + 1, 0).reshape(-1)
    owner, local, valid = flatten(n_ce, w_max)
    c_chunk = owner // ne
    c_tile = jnp.clip(s_lo.reshape(-1)[owner] + local, 0, nt_max - 1)
    per_chunk = jnp.sum(n_ce.reshape(nchunk, ne), axis=1)
    chunk_end = jnp.cumsum(per_chunk)
    wi = jnp.clip(jnp.arange(w_max, dtype=I32), 0, jnp.maximum(chunk_end[-1] - 1, 0))
    first = wi == (chunk_end - per_chunk)[c_chunk]
    last = wi == chunk_end[c_chunk] - 1
    c_flags = (valid * (1 + 2 * first + 4 * last)).astype(I32)
    return (f_tile.astype(I32), f_chunk.astype(I32), f_flags, te, off.astype(I32),
            c_chunk.astype(I32), c_tile.astype(I32), c_flags, nt_max, w_max)


def moe(x, g, w_router, b_router, wg, wu, wd, tc=MOE_CHUNK, ts=MOE_SLOT_TILE):
    t, d = x.shape
    ne, _, f = wg.shape
    tc = min(tc, t)
    h, idx, gate, cend = moe_route(x, g, w_router, b_router, tc)
    (f_tile, f_chunk, f_flags, te, off, c_chunk, c_tile, c_flags, nt_max, w_max) = _work_lists(
        cend[:, :, 0], t, tc, ts, ne)

    sorted_out = pl.pallas_call(
        functools.partial(_moe_ffn_kernel, ts=ts, ne=ne),
        grid_spec=pltpu.PrefetchScalarGridSpec(
            num_scalar_prefetch=5,
            grid=(w_max,),
            in_specs=[
                pl.BlockSpec((ne, tc), lambda w, wt, wc, wf, te_, of: (0, wc[w])),
                pl.BlockSpec((tc, d), lambda w, wt, wc, wf, te_, of: (wc[w], 0)),
                pl.BlockSpec((None, d, f), lambda w, wt, wc, wf, te_, of: (te_[wt[w]], 0, 0)),
                pl.BlockSpec((None, d, f), lambda w, wt, wc, wf, te_, of: (te_[wt[w]], 0, 0)),
                pl.BlockSpec((None, f, d), lambda w, wt, wc, wf, te_, of: (te_[wt[w]], 0, 0)),
            ],
            out_specs=pl.BlockSpec((ts, d), lambda w, wt, wc, wf, te_, of: (wt[w], 0)),
            scratch_shapes=[pltpu.VMEM((ts, d), F32)],
        ),
        out_shape=jax.ShapeDtypeStruct((nt_max * ts, d), BF16),
        compiler_params=_params("arbitrary"),
        name="moe_ffn",
    )(f_tile, f_chunk, f_flags, te, off, idx, h, wg, wu, wd)

    return pl.pallas_call(
        functools.partial(_moe_combine_kernel, ts=ts, ne=ne),
        grid_spec=pltpu.PrefetchScalarGridSpec(
            num_scalar_prefetch=4,
            grid=(w_max,),
            in_specs=[
                pl.BlockSpec((ne, tc), lambda w, wc, wt, wf, of: (0, wc[w])),
                pl.BlockSpec((ne, tc), lambda w, wc, wt, wf, of: (0, wc[w])),
                pl.BlockSpec((tc, d), lambda w, wc, wt, wf, of: (wc[w], 0)),
                pl.BlockSpec((ts, d), lambda w, wc, wt, wf, of: (wt[w], 0)),
            ],
            out_specs=pl.BlockSpec((tc, d), lambda w, wc, wt, wf, of: (wc[w], 0)),
            scratch_shapes=[pltpu.VMEM((tc, d), F32)],
        ),
        out_shape=jax.ShapeDtypeStruct((t, d), F32),
        compiler_params=_params("arbitrary"),
        name="moe_combine",
    )(c_chunk, c_tile, c_flags, off, idx, gate, x, sorted_out)


def _lambda_init(layer):
    return 0.8 - 0.6 * math.exp(-0.3 * layer)


def kernel(x_prompt, x_sample, state_ssm, state_conv, cache_k, cache_v, page_table, norm_mix_g, norm_ffn_g, norm_kv_g, norm_final_g, ssm_w_in, ssm_conv_w, ssm_conv_b, ssm_dt_bias, ssm_a_log, ssm_d, ssm_norm_g, ssm_w_out, kv_w_k, kv_w_v, att_w_q, att_lam_q1, att_lam_k1, att_lam_q2, att_lam_k2, att_subln_g, att_w_o, ffn_w_gate, ffn_w_up, ffn_w_down, moe_w_router, moe_b_router, moe_w_gate, moe_w_up, moe_w_down):
    depth, d_model = norm_mix_g.shape
    n_a = ssm_w_in.shape[0]
    heads_ssm = ssm_a_log.shape[1]
    d_inner = heads_ssm * SSM_HEAD_DIM
    conv_dim = ssm_conv_w.shape[2]
    att_dv = att_subln_g.shape[1]
    att_heads = kv_w_v.shape[1] // att_dv
    slopes = 2.0 ** (-8.0 * jnp.arange(1, att_heads + 1, dtype=F32) / att_heads)

    def per_layer(w):
        return [w[i].astype(BF16) for i in range(w.shape[0])]

    in_dim = ssm_w_in.shape[2]
    w_in = per_layer(jnp.pad(ssm_w_in, ((0, 0), (0, 0), (0, d_inner + conv_dim + LANE - in_dim))))
    w_out = per_layer(ssm_w_out)
    w_k, w_v = kv_w_k.astype(BF16), kv_w_v.astype(BF16)
    w_q, w_o = per_layer(att_w_q), per_layer(att_w_o)
    f_gate, f_up, f_down = per_layer(ffn_w_gate), per_layer(ffn_w_up), per_layer(ffn_w_down)
    m_gate, m_up, m_down = per_layer(moe_w_gate), per_layer(moe_w_up), per_layer(moe_w_down)
    lam_vecs = jnp.stack([att_lam_q1, att_lam_k1, att_lam_q2, att_lam_k2], axis=1)

    def trunk(x3, ssm0, conv0, past):
        b, l, _ = x3.shape
        x = x3.reshape(b * l, d_model)
        states = None
        k_new = v_new = k_att = v_att = None
        for layer in range(depth):
            if layer < n_a:
                ssd_w = (ssm_conv_w[layer], ssm_conv_b[layer], ssm_dt_bias[layer], ssm_a_log[layer],
                         ssm_d[layer], ssm_norm_g[layer])
                if l % SSD_CHUNK == 0:
                    xo, *states = ssd_mixer(x.reshape(b, l, d_model), layer, n_a, conv0, ssm0, states, *ssd_w,
                                            SSD_CHUNK, fuse=(norm_mix_g[layer], w_in[layer], w_out[layer]))
                    x = xo.reshape(b * l, d_model)
                else:
                    zx = linear(x, w_in[layer], g=norm_mix_g[layer]).reshape(b, l, -1)
                    y, *states = ssd_mixer(zx, layer, n_a, conv0, ssm0, states, *ssd_w, l)
                    x = linear(y.reshape(b * l, d_inner), w_out[layer], residual=x)
            else:
                j = layer - n_a
                lam_init = _lambda_init(layer)
                if past is None:
                    qt = linear(x, w_q[j], g=norm_mix_g[layer], scale=ATT_HEAD_DIM ** -0.5, outs=("bf16_t",),
                                tm=ATT_TILE)
                    o = attention_prompt(qt.reshape(b, l // ATT_TILE, -1, ATT_TILE), k_att, v_att,
                                         lam_vecs[j], att_subln_g[j], slopes, lam_init)
                else:
                    q = linear(x, w_q[j], g=norm_mix_g[layer])
                    o = attention_decode(q.reshape(b, l, -1), k_att, v_att, past[0], past[1], page_table,
                                         lam_vecs[j], att_subln_g[j], slopes, lam_init)
                x = linear(o.reshape(b * l, -1), w_o[j], residual=x)
            i = layer // 2
            if layer % 2 == 0:
                x = ffn(x, norm_ffn_g[layer], f_gate[i], f_up[i], f_down[i])
            else:
                x = moe(x, norm_ffn_g[layer], moe_w_router[i], moe_b_router[i], m_gate[i], m_up[i], m_down[i])
            if layer == n_a - 1:
                if past is None:
                    assert l % ATT_TILE == 0
                    k_new, k_att = linear(x, w_k, g=norm_kv_g, outs=("f32", "bf16"))
                    v_new, v_att = linear(x, w_v, g=norm_kv_g, outs=("f32", "bf16_t"), tm=ATT_TILE)
                    k_att = k_att.reshape(b, l, -1)
                    v_att = v_att.reshape(b, l // ATT_TILE, -1, ATT_TILE)
                else:
                    k_att = k_new = linear(x, w_k, g=norm_kv_g).reshape(b, l, -1)
                    v_att = v_new = linear(x, w_v, g=norm_kv_g).reshape(b, l, -1)
        y = rmsnorm(x, norm_final_g).reshape(b, l, d_model)
        return (y, states[0].reshape(n_a, b, heads_ssm, SSM_HEAD_DIM, SSM_STATE), states[1],
                k_new.reshape(b, l, att_heads, att_dv), v_new.reshape(b, l, att_heads, att_dv))

    y_p, ssm_p, conv_p, k_p, v_p = trunk(x_prompt, None, None, None)
    bs = x_sample.shape[0]
    y_s, ssm_s, conv_s, k_s, v_s = trunk(x_sample, state_ssm.reshape(n_a, bs, d_inner, SSM_STATE), state_conv,
                                         (cache_k, cache_v))
    return (y_p, y_s, ssm_p, conv_p, k_p, v_p, ssm_s, conv_s, k_s, v_s)
```

```python
import functools
import math

import jax
import jax.numpy as jnp
from jax import lax
from jax.experimental import pallas as pl
from jax.experimental.pallas import tpu as pltpu

F32 = jnp.float32
BF16 = jnp.bfloat16
I32 = jnp.int32

RMS_EPS = 1e-5
SSM_HEAD_DIM = 64
SSM_GROUPS = 8
SSM_STATE = 128
CONV_WIDTH = 4
SSD_CHUNK = 128
ATT_HEAD_DIM = 64
N_EXPERTS = 8
LANE = 128
SUBLANE = 8
VMEM_LIMIT = 56 * 1024 * 1024
NEG = -1e30

TOKEN_TILE = 256
MOE_SLOT_TILE = 256
MOE_CHUNK = 1024
ATT_TILE = 256
PAGES_PER_STEP = 16
DECODE_GROUP = 4
SSD_SEQS_PER_STEP = 4


def _params(*sem):
    return pltpu.CompilerParams(dimension_semantics=sem, vmem_limit_bytes=VMEM_LIMIT)


def _resident(shape):
    nd = len(shape)
    return pl.BlockSpec(shape, lambda *_: (0,) * nd, pipeline_mode=pl.Buffered(1))


def _weight(w):
    if isinstance(w, tuple):
        stack, i = w
        shape = stack.shape[1:]
        zeros = (0,) * len(shape)
        return stack, shape, pl.BlockSpec((None,) + shape, lambda *_: (i,) + zeros,
                                          pipeline_mode=pl.Buffered(1))
    return w, w.shape, _resident(w.shape)


def _rms(x, g):
    ms = jnp.mean(x * x, axis=-1, keepdims=True)
    return x * lax.rsqrt(ms + RMS_EPS) * g


def _silu(x):
    return (0.5 * x) * (1.0 + jnp.tanh(0.5 * x))


def _dot(a, b):
    return jnp.dot(a, b, preferred_element_type=F32)


def _dot_nt(a, b):
    return lax.dot_general(a, b, (((1,), (1,)), ((), ())), preferred_element_type=F32)


def _dot_tn(a, b):
    return lax.dot_general(a, b, (((0,), (0,)), ((), ())), preferred_element_type=F32)


def _split2(x):
    hi = x.astype(BF16)
    lo = (x - hi.astype(F32)).astype(BF16)
    return hi, lo


def _split3(x):
    hi = x.astype(BF16)
    r = x - hi.astype(F32)
    mid = r.astype(BF16)
    lo = (r - mid.astype(F32)).astype(BF16)
    return hi, mid, lo


def _linear_kernel(*refs, norm, residual, scale, outs):
    it = iter(refs)
    x_ref = next(it)
    g_ref = next(it) if norm else None
    w_ref = next(it)
    r_ref = next(it) if residual else None
    x = x_ref[...].astype(F32)
    if norm:
        x = _rms(x, g_ref[...])
    y = _dot(x.astype(BF16), w_ref[...])
    if residual:
        y = y + r_ref[...]
    if scale is not None:
        y = y * scale
    for kind in outs:
        o_ref = next(it)
        if kind == "f32":
            o_ref[...] = y
        elif kind == "bf16":
            o_ref[...] = y.astype(BF16)
        else:
            o_ref[...] = jnp.transpose(y).astype(BF16)


def linear(x, w, g=None, residual=None, scale=None, outs=("f32",), tm=TOKEN_TILE):
    t, k = x.shape
    w, (_, n), w_spec = _weight(w)
    tm = min(tm, t)
    assert t % tm == 0
    args = [x]
    specs = [pl.BlockSpec((tm, k), lambda i: (i, 0))]
    if g is not None:
        args.append(g.reshape(1, k))
        specs.append(_resident((1, k)))
    args.append(w)
    specs.append(w_spec)
    if residual is not None:
        args.append(residual)
        specs.append(pl.BlockSpec((tm, n), lambda i: (i, 0)))
    out_specs, out_shapes = [], []
    for kind in outs:
        if kind == "bf16_t":
            out_specs.append(pl.BlockSpec((None, n, tm), lambda i: (i, 0, 0)))
            out_shapes.append(jax.ShapeDtypeStruct((t // tm, n, tm), BF16))
        else:
            out_specs.append(pl.BlockSpec((tm, n), lambda i: (i, 0)))
            out_shapes.append(jax.ShapeDtypeStruct((t, n), F32 if kind == "f32" else BF16))
    res = pl.pallas_call(
        functools.partial(_linear_kernel, norm=g is not None, residual=residual is not None,
                          scale=scale, outs=tuple(outs)),
        grid=(t // tm,),
        in_specs=specs,
        out_specs=out_specs,
        out_shape=out_shapes,
        compiler_params=_params("parallel"),
        name="linear",
    )(*args)
    return res[0] if len(outs) == 1 else res


def _ffn_kernel(x_ref, g_ref, wg_ref, wu_ref, wd_ref, o_ref):
    x = x_ref[...]
    h = _rms(x, g_ref[...]).astype(BF16)
    a = _dot(h, wg_ref[...])
    u = _dot(h, wu_ref[...])
    act = (_silu(a) * u).astype(BF16)
    o_ref[...] = x + _dot(act, wd_ref[...])


def ffn(x, g, wg, wu, wd, tm=TOKEN_TILE):
    t, d = x.shape
    (wg, _, wg_spec), (wu, _, wu_spec), (wd, _, wd_spec) = _weight(wg), _weight(wu), _weight(wd)
    tm = min(tm, t)
    assert t % tm == 0
    return pl.pallas_call(
        _ffn_kernel,
        grid=(t // tm,),
        in_specs=[pl.BlockSpec((tm, d), lambda i: (i, 0)), _resident((1, d)), wg_spec, wu_spec, wd_spec],
        out_specs=pl.BlockSpec((tm, d), lambda i: (i, 0)),
        out_shape=jax.ShapeDtypeStruct((t, d), F32),
        compiler_params=_params("parallel"),
        name="ffn",
    )(x, g.reshape(1, d), wg, wu, wd)


def _rmsnorm_kernel(x_ref, g_ref, o_ref):
    o_ref[...] = _rms(x_ref[...], g_ref[...])


def rmsnorm(x, g, tm=TOKEN_TILE):
    t, d = x.shape
    tm = min(tm, t)
    assert t % tm == 0
    return pl.pallas_call(
        _rmsnorm_kernel,
        grid=(t // tm,),
        in_specs=[pl.BlockSpec((tm, d), lambda i: (i, 0)), _resident((1, d))],
        out_specs=pl.BlockSpec((tm, d), lambda i: (i, 0)),
        out_shape=jax.ShapeDtypeStruct((t, d), F32),
        compiler_params=_params("parallel"),
        name="rmsnorm",
    )(x, g.reshape(1, d))


def _ssd_kernel(*refs, q, rows, valid, d_inner, heads, fused, has_init, has_prev, single_chunk):
    it = iter(refs)
    src_ref = next(it)
    if fused:
        gmix_ref, win_ref, wout_ref = next(it), next(it), next(it)
    if has_init:
        conv0_ref, ssm0_ref = next(it), next(it)
    cw_ref, cb_ref, dtb_ref, alog_ref, dsk_ref, ng_ref, exp_ref = (next(it) for _ in range(7))
    if has_prev:
        next(it), next(it)
    out_ref, sfin_ref, cnew_ref, xp_ref, stage_ref = (next(it) for _ in range(5))
    ybuf_ref = next(it) if fused else None

    c = pl.program_id(1)
    nc = pl.num_programs(1)
    kp = SSD_CHUNK
    gn = SSM_GROUPS * SSM_STATE
    hpg = heads // SSM_GROUPS
    gw = hpg * SSM_HEAD_DIM
    halo = CONV_WIDTH - 1

    def when(cond):
        return (lambda f: f()) if single_chunk else pl.when(cond)

    @when(c == 0)
    def _():
        xp_ref[0:SUBLANE, :] = jnp.zeros((SUBLANE, xp_ref.shape[1]), F32)
        if has_init:
            sfin_ref[...] = ssm0_ref[...]
            xp_ref[SUBLANE - halo:SUBLANE, :] = conv0_ref[...]
        else:
            sfin_ref[...] = jnp.zeros(sfin_ref.shape, F32)

    if rows < q:
        stage_ref[...] = jnp.zeros(stage_ref.shape, F32)
        stage_ref[0:rows, :] = src_ref[...]
        src = stage_ref[...]
    else:
        src = src_ref[...]
    if fused:
        zx = _dot(_rms(src, gmix_ref[...]).astype(BF16), win_ref[...])
    else:
        zx = src
    z = zx[:, :d_inner]
    xbc = zx[:, d_inner:2 * d_inner + 2 * gn]
    dtr = zx[:, 2 * d_inner + 2 * gn:]

    xp_ref[SUBLANE:SUBLANE + q, :] = xbc
    xall = xp_ref[...]
    conv = cb_ref[...] + xall[SUBLANE:] * cw_ref[halo:halo + 1, :]
    for shift in range(1, CONV_WIDTH):
        k = halo - shift
        conv = conv + pltpu.roll(xall, shift, 0)[SUBLANE:] * cw_ref[k:k + 1, :]
    tail = xp_ref[SUBLANE + valid - halo:SUBLANE + valid, :]
    xp_ref[SUBLANE - halo:SUBLANE, :] = tail

    @when(c == nc - 1)
    def _():
        cnew_ref[...] = tail

    xa = _silu(conv)
    xs = xa[:, :d_inner]
    bm = xa[:, d_inner:d_inner + gn]
    cm = xa[:, d_inner + gn:]

    def pad(v):
        if q == kp:
            return v
        return jnp.concatenate([v, jnp.zeros((kp - q, v.shape[1]), v.dtype)], axis=0)

    row = lax.broadcasted_iota(I32, (q, kp), 0)
    col = lax.broadcasted_iota(I32, (q, kp), 1)
    causal = col <= row

    dtv = dtr + dtb_ref[...]
    dt = jnp.maximum(dtv, 0.0) + jnp.log1p(jnp.exp(-jnp.abs(dtv)))
    if valid < q:
        dt = jnp.where(lax.broadcasted_iota(I32, (q, LANE), 0) < valid, dt, 0.0)
    adt = dt * (-jnp.exp(alog_ref[...]))
    tril = causal.astype(BF16)
    cs = sum(_dot(tril, part) for part in _split3(pad(adt)))
    cs_t = jnp.transpose(pad(cs))
    cs_last = cs[q - 1:q, :]
    e_last = jnp.exp(cs_last)

    stacked = jnp.concatenate([dt, jnp.exp(cs), jnp.exp(cs_last - cs)], axis=0)
    hi, lo = _split2(stacked)
    wide = _dot(hi, exp_ref[...]) + _dot(lo, exp_ref[...])
    dt_w, ecs_w, dout_w = wide[0:q], wide[q:2 * q], wide[2 * q:3 * q]
    xdt = xs * dt_w
    xdtd = pad(xdt * dout_w).astype(BF16)
    xdt_b = pad(xdt).astype(BF16)
    bm_p = pad(bm).astype(BF16)
    cm_b = cm.astype(BF16)
    lane_g = lax.broadcasted_iota(I32, (kp, gw), 1) // SSM_HEAD_DIM
    head_mask = [(lane_g == r).astype(BF16) for r in range(hpg)]

    for g in range(SSM_GROUPS):
        b_g = bm_p[:, g * SSM_STATE:(g + 1) * SSM_STATE]
        c_g = cm_b[:, g * SSM_STATE:(g + 1) * SSM_STATE]
        cb = _dot_nt(c_g, b_g)
        xg = xdt_b[:, g * gw:(g + 1) * gw]
        m_parts, x_parts, scale_parts = [], [], []
        for r in range(hpg):
            h = g * hpg + r
            seg = cs[:, h:h + 1] - cs_t[h:h + 1, :]
            decay = jnp.exp(jnp.where(causal, seg, NEG))
            m_parts.append((cb * decay).astype(BF16))
            x_parts.append(xg * head_mask[r])
            scale_parts.append(jnp.broadcast_to(e_last[:, h:h + 1], (SSM_HEAD_DIM, SSM_STATE)))
        y_diag = _dot(jnp.concatenate(m_parts, axis=1), jnp.concatenate(x_parts, axis=0))
        s_g = sfin_ref[g * gw:(g + 1) * gw, :]
        y_off = _dot_nt(c_g, s_g.astype(BF16)) * ecs_w[:, g * gw:(g + 1) * gw]
        new = _dot_tn(xdtd[:, g * gw:(g + 1) * gw], b_g)
        sfin_ref[g * gw:(g + 1) * gw, :] = s_g * jnp.concatenate(scale_parts, axis=0) + new
        y = y_diag + y_off + dsk_ref[:, g * gw:(g + 1) * gw] * xs[:, g * gw:(g + 1) * gw]
        y = y * _silu(z[:, g * gw:(g + 1) * gw])
        y = y * lax.rsqrt(jnp.mean(y * y, axis=-1, keepdims=True) + RMS_EPS)
        y = y * ng_ref[:, g * gw:(g + 1) * gw]
        if fused:
            ybuf_ref[:, g * gw:(g + 1) * gw] = y.astype(BF16)
        else:
            out_ref[:, g * gw:(g + 1) * gw] = y

    if fused:
        out_ref[...] = src + _dot(ybuf_ref[...], wout_ref[...])


def _ssd_step_kernel(*refs, nb, per_seq, **kw):
    for sb in range(nb):
        _ssd_kernel(*[r.at[sb] if own else r for r, own in zip(refs, per_seq)], **kw)


def ssd_mixer(src, layer, n_layers, conv0_all, ssm0_all, prev, conv_w, conv_b, dt_bias, a_log, d_skip,
              norm_g, valid, fuse=None):
    b, l, width = src.shape
    heads = a_log.shape[0]
    d_inner = heads * SSM_HEAD_DIM
    conv_dim = conv_w.shape[1]
    if l % SSD_CHUNK == 0:
        q = rows = SSD_CHUNK
        nc = l // SSD_CHUNK
        assert valid == SSD_CHUNK
    else:
        assert l < SSD_CHUNK
        rows, nc = l, 1
        q = -(-l // SUBLANE) * SUBLANE
    padh = LANE - heads
    expand = jnp.repeat(jnp.eye(LANE, heads, dtype=BF16), SSM_HEAD_DIM, axis=1)
    halo = CONV_WIDTH - 1

    nb = math.gcd(SSD_SEQS_PER_STEP, b) if nc == 1 else 1
    args = [src]
    specs = [pl.BlockSpec((nb, rows, width), lambda i, j: (i, j, 0))]
    per_seq = [True]
    if fuse is not None:
        g_mix, w_in, w_out = fuse
        (w_in, in_shape, in_spec), (w_out, out_shape, out_spec) = _weight(w_in), _weight(w_out)
        assert in_shape == (width, d_inner + conv_dim + LANE)
        args += [g_mix.reshape(1, width), w_in, w_out]
        specs += [_resident((1, width)), in_spec, out_spec]
        per_seq += [False] * 3
        out_width = out_shape[1]
    else:
        assert width == d_inner + conv_dim + LANE
        out_width = d_inner
    if ssm0_all is not None:
        args += [conv0_all, ssm0_all]
        specs += [pl.BlockSpec((None, nb, halo, conv_dim), lambda i, j: (layer, i, 0, 0)),
                  pl.BlockSpec((None, nb, d_inner, SSM_STATE), lambda i, j: (layer, i, 0, 0))]
        per_seq += [True] * 2
    args += [conv_w, conv_b.reshape(1, conv_dim),
             jnp.pad(dt_bias, (0, padh)).reshape(1, LANE), jnp.pad(a_log, (0, padh)).reshape(1, LANE),
             jnp.repeat(d_skip, SSM_HEAD_DIM).reshape(1, d_inner), norm_g.reshape(1, d_inner), expand]
    specs += [_resident((CONV_WIDTH, conv_dim)), _resident((1, conv_dim)), _resident((1, LANE)),
              _resident((1, LANE)), _resident((1, d_inner)), _resident((1, d_inner)),
              _resident((LANE, d_inner))]
    per_seq += [False] * 7
    aliases = {}
    if prev is not None:
        aliases = {len(args): 1, len(args) + 1: 2}
        args += list(prev)
        specs += [pl.BlockSpec(memory_space=pl.ANY), pl.BlockSpec(memory_space=pl.ANY)]
        per_seq += [False] * 2
    per_seq += [True] * 3
    scratch = [pltpu.VMEM((nb, SUBLANE + q, conv_dim), F32),
               pltpu.VMEM((nb, q, width) if rows < q else (nb, SUBLANE, LANE), F32)]
    per_seq += [True] * 2
    if fuse is not None:
        scratch.append(pltpu.VMEM((q, d_inner), BF16))
        per_seq.append(False)
    kern = functools.partial(_ssd_step_kernel, nb=nb, per_seq=tuple(per_seq), q=q, rows=rows, valid=valid,
                             d_inner=d_inner, heads=heads, fused=fuse is not None,
                             has_init=ssm0_all is not None, has_prev=prev is not None, single_chunk=nc == 1)
    out, sfin, cnew = pl.pallas_call(
        kern,
        grid=(b // nb, nc),
        in_specs=specs,
        out_specs=[
            pl.BlockSpec((nb, q, out_width), lambda i, j: (i, j, 0)),
            pl.BlockSpec((None, nb, d_inner, SSM_STATE), lambda i, j: (layer, i, 0, 0)),
            pl.BlockSpec((None, nb, halo, conv_dim), lambda i, j: (layer, i, 0, 0)),
        ],
        out_shape=[
            jax.ShapeDtypeStruct((b, nc * q, out_width), F32),
            jax.ShapeDtypeStruct((n_layers, b, d_inner, SSM_STATE), F32),
            jax.ShapeDtypeStruct((n_layers, b, halo, conv_dim), F32),
        ],
        scratch_shapes=scratch,
        input_output_aliases=aliases,
        compiler_params=_params("parallel", "arbitrary"),
        name="ssd_mixer",
    )(*args)
    return out[:, :l], sfin, cnew


def _lambda(lam_ref, lam_init):
    lv = lam_ref[...]
    s1 = jnp.sum(lv[0:1] * lv[1:2], axis=-1, keepdims=True)
    s2 = jnp.sum(lv[2:3] * lv[3:4], axis=-1, keepdims=True)
    return jnp.exp(s1) - jnp.exp(s2) + lam_init


def _two_branch_q(qv, n):
    lane = lax.broadcasted_iota(I32, qv.shape, 1)
    zero = jnp.zeros_like(qv)
    return jnp.concatenate([jnp.where(lane < ATT_HEAD_DIM, qv, zero),
                            jnp.where(lane >= ATT_HEAD_DIM, qv, zero)], axis=0).astype(BF16)


def _attn_prompt_kernel(slope_ref, qt_ref, k_ref, vt_ref, lam_ref, sg_ref, o_ref, *, tq, lam_init):
    slope = slope_ref[pl.program_id(1)]
    nq, dv, _ = qt_ref.shape
    feat = lax.broadcasted_iota(I32, (dv, tq), 0)
    cc = lax.broadcasted_iota(I32, (tq, 2 * tq), 1)
    rel = jnp.where(cc >= tq, cc - tq, cc) - lax.broadcasted_iota(I32, (tq, 2 * tq), 0)
    bias = -slope * rel.astype(F32)
    bias_diag = jnp.where(rel >= 0, bias, NEG)
    lam = _lambda(lam_ref, lam_init)
    m0 = jnp.full((1, 2 * tq), NEG, F32)
    l0 = jnp.zeros((1, 2 * tq), F32)
    a0 = jnp.zeros((dv, 2 * tq), F32)

    def q_tile(qi, _):
        qt = qt_ref[qi]
        zero = jnp.zeros_like(qt)
        q2t = jnp.concatenate([jnp.where(feat < ATT_HEAD_DIM, qt, zero),
                               jnp.where(feat >= ATT_HEAD_DIM, qt, zero)], axis=1)

        def scores(j):
            start = pl.multiple_of(j * tq, tq)
            return _dot(k_ref[pl.ds(start, tq), :], q2t)

        def consume(j, s, carry):
            m, l, acc = carry
            tile_bias = -slope * ((qi - j) * tq).astype(F32)
            m_new = jnp.maximum(m, jnp.max(s, axis=0, keepdims=True) + tile_bias)
            alpha = jnp.exp(m - m_new)
            p = jnp.exp(s - (m_new - tile_bias))
            l = alpha * l + jnp.sum(p, axis=0, keepdims=True)
            acc = alpha * acc + _dot(vt_ref[j], p.astype(BF16))
            return m_new, l, acc

        def body(j, c):
            s_next = scores(j + 1)
            return consume(j, c[3] + bias, c[:3]) + (s_next,)

        c = lax.fori_loop(0, qi, body, (m0, l0, a0, scores(0)))
        _, l, acc = consume(qi, c[3] + bias_diag, c[:3])
        o = acc / l
        o = jnp.transpose(o[:, :tq] - lam * o[:, tq:])
        o_ref[pl.ds(pl.multiple_of(qi * tq, tq), tq), :] = _rms(o, sg_ref[...]) * (1.0 - lam_init)
        return 0

    lax.fori_loop(0, nq, q_tile, 0)


def attention_prompt(qt, k, vt, lam_vecs, subln_g, slopes, lam_init):
    b, nq, hd, tq = qt.shape
    l = nq * tq
    dv = subln_g.shape[0]
    heads = hd // dv
    return pl.pallas_call(
        functools.partial(_attn_prompt_kernel, tq=tq, lam_init=lam_init),
        grid_spec=pltpu.PrefetchScalarGridSpec(
            num_scalar_prefetch=1,
            grid=(b, heads),
            in_specs=[
                pl.BlockSpec((None, nq, dv, tq), lambda i, h, s: (i, 0, h, 0)),
                pl.BlockSpec((None, l, dv), lambda i, h, s: (i, 0, h)),
                pl.BlockSpec((None, nq, dv, tq), lambda i, h, s: (i, 0, h, 0)),
                pl.BlockSpec(lam_vecs.shape, lambda i, h, s: (0, 0)),
                pl.BlockSpec((1, dv), lambda i, h, s: (0, 0)),
            ],
            out_specs=pl.BlockSpec((None, l, dv), lambda i, h, s: (i, 0, h)),
        ),
        out_shape=jax.ShapeDtypeStruct((b, l, hd), F32),
        compiler_params=_params("parallel", "parallel"),
        name="attention_prompt",
    )(slopes, qt, k, vt, lam_vecs, subln_g.reshape(1, dv))


def _attn_decode_kernel(pt_ref, slope_ref, q_ref, kn_ref, vn_ref, lam_ref, sg_ref, table_ref, *rest,
                        pp, group, heads, page, n_new, lam_init):
    k_refs = rest[:pp]
    v_refs = rest[pp:2 * pp]
    o_ref = rest[2 * pp]
    m_ref, l_ref, acc_ref = rest[2 * pp + 1:]
    j = pl.program_id(1)
    nj = pl.num_programs(1)
    dv = sg_ref.shape[1]
    nr = 2 * SUBLANE
    pairs = page * heads
    slope_col = jnp.concatenate([jnp.full((nr, 1), slope_ref[h], F32) for h in range(heads)], axis=0)

    @pl.when(j == 0)
    def _():
        m_ref[...] = jnp.full(m_ref.shape, NEG, F32)
        l_ref[...] = jnp.zeros(l_ref.shape, F32)
        acc_ref[...] = jnp.zeros(acc_ref.shape, F32)

    qv = q_ref[...] * (ATT_HEAD_DIM ** -0.5)
    q2 = [_two_branch_q(qv[:, h * dv:(h + 1) * dv], SUBLANE) for h in range(heads)]
    q_all = jnp.concatenate(q2, axis=0)

    def update(s, shift, pv_of):
        m = m_ref[...]
        m_new = jnp.maximum(m, jnp.max(s, axis=-1, keepdims=True) + shift)
        alpha = jnp.exp(m - m_new)
        pr = jnp.exp(s - (m_new - shift))
        l_ref[...] = alpha * l_ref[...] + jnp.sum(pr, axis=-1, keepdims=True)
        acc_ref[...] = alpha * acc_ref[...] + pv_of(pr.astype(BF16))
        m_ref[...] = m_new

    for g0 in range(0, pp, group):
        ks = [k_refs[p][...].astype(BF16) for p in range(g0, g0 + group)]
        vs = [v_refs[p][...].astype(BF16) for p in range(g0, g0 + group)]
        s = jnp.concatenate([_dot_nt(q_all, kb) for kb in ks], axis=1) + table_ref[...]
        start = ((j * pp + g0) * page).astype(F32)
        update(s, slope_col * start,
               lambda pb: sum(_dot(pb[:, i * pairs:(i + 1) * pairs], vs[i]) for i in range(group)))

    @pl.when(j == nj - 1)
    def _():
        row = lax.broadcasted_iota(I32, (heads * nr, page), 0)
        col = lax.broadcasted_iota(I32, (heads * nr, page), 1)
        zeros = jnp.zeros((page - SUBLANE, kn_ref.shape[1]), F32)
        kb = jnp.concatenate([kn_ref[...], zeros], axis=0).astype(BF16)
        vb = jnp.concatenate([vn_ref[...], zeros], axis=0).astype(BF16)
        dist = (row % SUBLANE) - col
        ok = (dist >= 0) & (col < n_new)
        s = jnp.concatenate([_dot_nt(q2[h], kb[:, h * dv:(h + 1) * dv]) for h in range(heads)], axis=0)
        update(s + jnp.where(ok, -slope_col * dist.astype(F32), NEG), jnp.zeros_like(slope_col),
               lambda pb: jnp.concatenate([_dot(pb[h * nr:(h + 1) * nr], vb[:, h * dv:(h + 1) * dv])
                                           for h in range(heads)], axis=0))
        o = acc_ref[...] / l_ref[...]
        lam = _lambda(lam_ref, lam_init)
        for h in range(heads):
            oh = o[h * nr:h * nr + SUBLANE] - lam * o[h * nr + SUBLANE:(h + 1) * nr]
            o_ref[:, h * dv:(h + 1) * dv] = _rms(oh, sg_ref[...]) * (1.0 - lam_init)


def attention_decode(q, k_new, v_new, cache_k, cache_v, page_table, lam_vecs, subln_g, slopes, lam_init):
    b, n_new, hd = q.shape
    dv = subln_g.shape[0]
    heads = hd // dv
    n_pages = page_table.shape[1]
    page = cache_k.shape[1]
    pp = math.gcd(PAGES_PER_STEP, n_pages)
    assert n_new <= SUBLANE
    padr = ((0, 0), (0, SUBLANE - n_new), (0, 0))
    qp, kp, vp = (jnp.pad(a, padr) for a in (q, k_new, v_new))

    n_pool = cache_k.shape[0]
    ck, cv = (c.reshape(n_pool, page * heads, dv) for c in (cache_k, cache_v))

    group = math.gcd(DECODE_GROUP, pp)
    nrow = heads * 2 * SUBLANE
    r = jnp.arange(nrow, dtype=I32)[:, None]
    c = jnp.arange(group * page * heads, dtype=I32)[None, :]
    qpos = (n_pages * page + r % SUBLANE).astype(F32)
    table = jnp.where(r // (2 * SUBLANE) == c % heads,
                      -slopes[r // (2 * SUBLANE)] * (qpos - (c // heads).astype(F32)), NEG)

    def page_spec(p_i):
        return pl.BlockSpec((None, page * heads, dv),
                            lambda i, j, pt, s: (pt[i * n_pages + j * pp + p_i], 0, 0))

    row_spec = pl.BlockSpec((None, SUBLANE, hd), lambda i, j, pt, s: (i, 0, 0))
    out = pl.pallas_call(
        functools.partial(_attn_decode_kernel, pp=pp, group=group, heads=heads, page=page, n_new=n_new,
                          lam_init=lam_init),
        grid_spec=pltpu.PrefetchScalarGridSpec(
            num_scalar_prefetch=2,
            grid=(b, n_pages // pp),
            in_specs=[row_spec, row_spec, row_spec,
                      pl.BlockSpec(lam_vecs.shape, lambda i, j, pt, s: (0, 0)),
                      pl.BlockSpec((1, dv), lambda i, j, pt, s: (0, 0)),
                      pl.BlockSpec(table.shape, lambda i, j, pt, s: (0, 0), pipeline_mode=pl.Buffered(1))]
                     + [page_spec(p_i) for p_i in range(pp)] * 2,
            out_specs=row_spec,
            scratch_shapes=[pltpu.VMEM((heads * 2 * SUBLANE, 1), F32),
                            pltpu.VMEM((heads * 2 * SUBLANE, 1), F32),
                            pltpu.VMEM((heads * 2 * SUBLANE, dv), F32)],
        ),
        out_shape=jax.ShapeDtypeStruct((b, SUBLANE, hd), F32),
        compiler_params=_params("parallel", "arbitrary"),
        name="attention_decode",
    )(page_table.reshape(-1), slopes, qp, kp, vp, lam_vecs, subln_g.reshape(1, dv), table,
      *([ck] * pp), *([cv] * pp))
    return out[:, :n_new]


def _router_kernel(x_ref, g_ref, wr_ref, br_ref, h_ref, idx_ref, gate_ref, cend_ref, run_ref):
    i = pl.program_id(0)
    tm = x_ref.shape[0]

    @pl.when(i == 0)
    def _():
        run_ref[...] = jnp.zeros(run_ref.shape, F32)

    h = _rms(x_ref[...], g_ref[...])
    h_ref[...] = h.astype(BF16)
    logits = lax.dot_general(wr_ref[...], h, (((1,), (1,)), ((), ())), precision=lax.Precision.HIGHEST,
                             preferred_element_type=F32) + br_ref[...]
    ne = logits.shape[0]
    eid = lax.broadcasted_iota(I32, (ne, tm), 0)
    eidf = eid.astype(F32)
    v0 = jnp.max(logits, axis=0, keepdims=True)
    i0 = jnp.min(jnp.where(logits == v0, eidf, float(ne)), axis=0, keepdims=True).astype(I32)
    rest = jnp.where(eid == i0, -jnp.inf, logits)
    v1 = jnp.max(rest, axis=0, keepdims=True)
    i1 = jnp.min(jnp.where(rest == v1, eidf, float(ne)), axis=0, keepdims=True).astype(I32)
    e = jnp.exp(v1 - v0)
    g0 = 1.0 / (1.0 + e)
    g1 = e / (1.0 + e)
    sel0 = eid == i0
    sel1 = eid == i1
    assign = (sel0 | sel1).astype(BF16)
    before = (lax.broadcasted_iota(I32, (tm, tm), 0) < lax.broadcasted_iota(I32, (tm, tm), 1)).astype(BF16)
    rank = _dot(assign, before) + run_ref[...]
    r0 = jnp.sum(jnp.where(sel0, rank, 0.0), axis=0, keepdims=True).astype(I32)
    r1 = jnp.sum(jnp.where(sel1, rank, 0.0), axis=0, keepdims=True).astype(I32)
    idx_ref[...] = jnp.where(eid == 0, i0, jnp.where(eid == 1, i1, jnp.where(eid == 2, r0,
                             jnp.where(eid == 3, r1, 0))))
    gate_ref[...] = jnp.where(eid == 0, g0, jnp.where(eid == 1, g1, 0.0))
    run = run_ref[...] + jnp.sum(assign.astype(F32), axis=1, keepdims=True)
    run_ref[...] = run
    cend_ref[...] = jnp.broadcast_to(run, cend_ref.shape)


def moe_route(x, g, w_router, b_router, tc=MOE_CHUNK):
    t, d = x.shape
    ne = w_router.shape[1]
    assert t % tc == 0 and ne == SUBLANE
    nchunk = t // tc
    return pl.pallas_call(
        _router_kernel,
        grid=(nchunk,),
        in_specs=[pl.BlockSpec((tc, d), lambda i: (i, 0)), _resident((1, d)),
                  _resident((ne, d)), _resident((ne, 1))],
        out_specs=[pl.BlockSpec((tc, d), lambda i: (i, 0)),
                   pl.BlockSpec((ne, tc), lambda i: (0, i)),
                   pl.BlockSpec((ne, tc), lambda i: (0, i)),
                   pl.BlockSpec((None, ne, LANE), lambda i: (i, 0, 0))],
        out_shape=[jax.ShapeDtypeStruct((t, d), BF16), jax.ShapeDtypeStruct((ne, t), I32),
                   jax.ShapeDtypeStruct((ne, t), F32), jax.ShapeDtypeStruct((nchunk, ne, LANE), F32)],
        scratch_shapes=[pltpu.VMEM((ne, 1), F32)],
        compiler_params=_params("arbitrary"),
        name="moe_route",
    )(x, g.reshape(1, d), w_router.T, b_router.reshape(ne, 1))


def _slot_rows(idx_ref, off_ref, slot0, ne):
    idx = idx_ref[...]
    e0, e1, r0, r1 = idx[0:1], idx[1:2], idx[2:3], idx[3:4]
    o0 = jnp.zeros_like(e0)
    o1 = jnp.zeros_like(e1)
    for k in range(ne):
        o0 = jnp.where(e0 == k, off_ref[k], o0)
        o1 = jnp.where(e1 == k, off_ref[k], o1)
    return o0 + r0 - slot0, o1 + r1 - slot0


def _moe_ffn_kernel(wt_ref, wc_ref, wf_ref, te_ref, off_ref, idx_ref, h_ref, wg_ref, wu_ref, wd_ref,
                    o_ref, acc_ref, *, ts, ne):
    w = pl.program_id(0)
    flags = wf_ref[w]

    @pl.when((flags & 2) != 0)
    def _():
        acc_ref[...] = jnp.zeros(acc_ref.shape, F32)

    @pl.when((flags & 1) != 0)
    def _():
        sa, sb = _slot_rows(idx_ref, off_ref, wt_ref[w] * ts, ne)
        rows = lax.broadcasted_iota(I32, (ts, idx_ref.shape[1]), 0)
        pick = ((rows == sa) | (rows == sb)).astype(BF16)
        acc_ref[...] += _dot(pick, h_ref[...])

    @pl.when((flags & 4) != 0)
    def _():
        xb = acc_ref[...].astype(BF16)
        a = _dot(xb, wg_ref[...])
        u = _dot(xb, wu_ref[...])
        o_ref[...] = _dot((_silu(a) * u).astype(BF16), wd_ref[...]).astype(BF16)


def _moe_combine_kernel(wc_ref, wt_ref, wf_ref, off_ref, idx_ref, gate_ref, x_ref, es_ref, *rest,
                        ts, ne, final_norm):
    fg_ref = rest[0] if final_norm else None
    o_ref, acc_ref = rest[-2:]
    w = pl.program_id(0)
    flags = wf_ref[w]

    @pl.when((flags & 2) != 0)
    def _():
        acc_ref[...] = x_ref[...]

    @pl.when((flags & 1) != 0)
    def _():
        sa, sb = _slot_rows(idx_ref, off_ref, wt_ref[w] * ts, ne)
        gt = gate_ref[...]
        rows = lax.broadcasted_iota(I32, (ts, idx_ref.shape[1]), 0)
        wgt = (jnp.where(rows == sa, gt[0:1], 0.0) + jnp.where(rows == sb, gt[1:2], 0.0)).astype(BF16)
        acc_ref[...] += _dot_tn(wgt, es_ref[...])

    @pl.when((flags & 4) != 0)
    def _():
        o_ref[...] = _rms(acc_ref[...], fg_ref[...]) if final_norm else acc_ref[...]


def _work_lists(cend, t, tc, ts, ne):
    nchunk = t // tc
    nt_max = (2 * t) // ts + ne
    w_max = nt_max + ne * nchunk
    cend = cend.astype(I32)
    cstart = jnp.concatenate([jnp.zeros((1, ne), I32), cend[:-1]], axis=0)
    cnt = cend[-1]
    tiles = (cnt + ts - 1) // ts
    tile_end = jnp.cumsum(tiles)
    off = (tile_end - tiles) * ts
    n_tiles = tile_end[-1]
    tile_ids = jnp.arange(nt_max, dtype=I32)

    def count_le(sorted_vals, queries):
        return jnp.sum(sorted_vals[None, :] <= queries[:, None], axis=1).astype(I32)

    te = jnp.minimum(count_le(tile_end, tile_ids), ne - 1)

    def flatten(counts, w_total):
        ends = jnp.cumsum(counts)
        total = ends[-1]
        wi = jnp.clip(jnp.arange(w_total, dtype=I32), 0, jnp.maximum(total - 1, 0))
        owner = jnp.minimum(count_le(ends, wi), counts.shape[0] - 1)
        local = wi - (ends[owner] - counts[owner])
        valid = jnp.arange(w_total, dtype=I32) < total
        return owner, local, valid

    k0 = tile_ids * ts - off[te]
    k1 = jnp.minimum(k0 + ts, cnt[te])
    ce_t = cend[:, te]
    cs_t = cstart[:, te]
    c_lo = jnp.sum(ce_t <= k0[None, :], axis=0).astype(I32)
    c_hi = (nchunk - 1 - jnp.sum(cs_t >= k1[None, :], axis=0)).astype(I32)
    active = tile_ids < n_tiles
    n_items = jnp.where(active, c_hi - c_lo + 1, 0)
    owner, local, valid = flatten(n_items, w_max)
    f_tile, f_chunk = owner, jnp.clip(c_lo[owner] + local, 0, nchunk - 1)
    first = local == 0
    last = local == n_items[owner] - 1
    f_flags = (valid * (1 + 2 * first + 4 * last)).astype(I32)

    has = cend > cstart
    s_lo = (off[None, :] + cstart) // ts
    s_hi = (off[None, :] + cend - 1) // ts
    n_ce = jnp.where(has, s_hi - s_lo + 1, 0).reshape(-1)
    owner, local, valid = flatten(n_ce, w_max)
    c_chunk = owner // ne
    c_tile = jnp.clip(s_lo.reshape(-1)[owner] + local, 0, nt_max - 1)
    per_chunk = jnp.sum(n_ce.reshape(nchunk, ne), axis=1)
    chunk_end = jnp.cumsum(per_chunk)
    wi = jnp.clip(jnp.arange(w_max, dtype=I32), 0, jnp.maximum(chunk_end[-1] - 1, 0))
    first = wi == (chunk_end - per_chunk)[c_chunk]
    last = wi == chunk_end[c_chunk] - 1
    c_flags = (valid * (1 + 2 * first + 4 * last)).astype(I32)
    return (f_tile.astype(I32), f_chunk.astype(I32), f_flags, te, off.astype(I32),
            c_chunk.astype(I32), c_tile.astype(I32), c_flags, nt_max, w_max)


def moe(x, g, w_router, b_router, wg, wu, wd, layer, final_g=None, tc=MOE_CHUNK, ts=MOE_SLOT_TILE):
    t, d = x.shape
    _, ne, _, f = wg.shape
    tc = min(tc, t)
    h, idx, gate, cend = moe_route(x, g, w_router, b_router, tc)
    (f_tile, f_chunk, f_flags, te, off, c_chunk, c_tile, c_flags, nt_max, w_max) = _work_lists(
        cend[:, :, 0], t, tc, ts, ne)

    sorted_out = pl.pallas_call(
        functools.partial(_moe_ffn_kernel, ts=ts, ne=ne),
        grid_spec=pltpu.PrefetchScalarGridSpec(
            num_scalar_prefetch=5,
            grid=(w_max,),
            in_specs=[
                pl.BlockSpec((ne, tc), lambda w, wt, wc, wf, te_, of: (0, wc[w])),
                pl.BlockSpec((tc, d), lambda w, wt, wc, wf, te_, of: (wc[w], 0)),
                pl.BlockSpec((None, None, d, f), lambda w, wt, wc, wf, te_, of: (layer, te_[wt[w]], 0, 0)),
                pl.BlockSpec((None, None, d, f), lambda w, wt, wc, wf, te_, of: (layer, te_[wt[w]], 0, 0)),
                pl.BlockSpec((None, None, f, d), lambda w, wt, wc, wf, te_, of: (layer, te_[wt[w]], 0, 0)),
            ],
            out_specs=pl.BlockSpec((ts, d), lambda w, wt, wc, wf, te_, of: (wt[w], 0)),
            scratch_shapes=[pltpu.VMEM((ts, d), F32)],
        ),
        out_shape=jax.ShapeDtypeStruct((nt_max * ts, d), BF16),
        compiler_params=_params("arbitrary"),
        name="moe_ffn",
    )(f_tile, f_chunk, f_flags, te, off, idx, h, wg, wu, wd)

    return pl.pallas_call(
        functools.partial(_moe_combine_kernel, ts=ts, ne=ne, final_norm=final_g is not None),
        grid_spec=pltpu.PrefetchScalarGridSpec(
            num_scalar_prefetch=4,
            grid=(w_max,),
            in_specs=[
                pl.BlockSpec((ne, tc), lambda w, wc, wt, wf, of: (0, wc[w])),
                pl.BlockSpec((ne, tc), lambda w, wc, wt, wf, of: (0, wc[w])),
                pl.BlockSpec((tc, d), lambda w, wc, wt, wf, of: (wc[w], 0)),
                pl.BlockSpec((ts, d), lambda w, wc, wt, wf, of: (wt[w], 0)),
            ] + ([pl.BlockSpec((1, d), lambda w, wc, wt, wf, of: (0, 0))] if final_g is not None else []),
            out_specs=pl.BlockSpec((tc, d), lambda w, wc, wt, wf, of: (wc[w], 0)),
            scratch_shapes=[pltpu.VMEM((tc, d), F32)],
        ),
        out_shape=jax.ShapeDtypeStruct((t, d), F32),
        compiler_params=_params("arbitrary"),
        name="moe_combine",
    )(c_chunk, c_tile, c_flags, off, idx, gate, x, sorted_out,
      *([final_g.reshape(1, d)] if final_g is not None else []))


def _lambda_init(layer):
    return 0.8 - 0.6 * math.exp(-0.3 * layer)


def kernel(x_prompt, x_sample, state_ssm, state_conv, cache_k, cache_v, page_table, norm_mix_g, norm_ffn_g, norm_kv_g, norm_final_g, ssm_w_in, ssm_conv_w, ssm_conv_b, ssm_dt_bias, ssm_a_log, ssm_d, ssm_norm_g, ssm_w_out, kv_w_k, kv_w_v, att_w_q, att_lam_q1, att_lam_k1, att_lam_q2, att_lam_k2, att_subln_g, att_w_o, ffn_w_gate, ffn_w_up, ffn_w_down, moe_w_router, moe_b_router, moe_w_gate, moe_w_up, moe_w_down):
    depth, d_model = norm_mix_g.shape
    n_a = ssm_w_in.shape[0]
    heads_ssm = ssm_a_log.shape[1]
    d_inner = heads_ssm * SSM_HEAD_DIM
    conv_dim = ssm_conv_w.shape[2]
    att_dv = att_subln_g.shape[1]
    att_heads = kv_w_v.shape[1] // att_dv
    slopes = 2.0 ** (-8.0 * jnp.arange(1, att_heads + 1, dtype=F32) / att_heads)

    def per_layer(w):
        stack = w.astype(BF16)
        return [(stack, i) for i in range(w.shape[0])]

    in_dim = ssm_w_in.shape[2]
    w_in = per_layer(jnp.pad(ssm_w_in, ((0, 0), (0, 0), (0, d_inner + conv_dim + LANE - in_dim))))
    w_out = per_layer(ssm_w_out)
    w_k, w_v = kv_w_k.astype(BF16), kv_w_v.astype(BF16)
    w_q, w_o = per_layer(att_w_q), per_layer(att_w_o)
    f_gate, f_up, f_down = per_layer(ffn_w_gate), per_layer(ffn_w_up), per_layer(ffn_w_down)
    m_gate, m_up, m_down = moe_w_gate.astype(BF16), moe_w_up.astype(BF16), moe_w_down.astype(BF16)
    lam_vecs = jnp.stack([att_lam_q1, att_lam_k1, att_lam_q2, att_lam_k2], axis=1)

    def trunk(x3, ssm0, conv0, past):
        b, l, _ = x3.shape
        x = x3.reshape(b * l, d_model)
        states = None
        k_new = v_new = k_att = v_att = None
        for layer in range(depth):
            if layer < n_a:
                ssd_w = (ssm_conv_w[layer], ssm_conv_b[layer], ssm_dt_bias[layer], ssm_a_log[layer],
                         ssm_d[layer], ssm_norm_g[layer])
                if l % SSD_CHUNK == 0:
                    xo, *states = ssd_mixer(x.reshape(b, l, d_model), layer, n_a, conv0, ssm0, states, *ssd_w,
                                            SSD_CHUNK, fuse=(norm_mix_g[layer], w_in[layer], w_out[layer]))
                    x = xo.reshape(b * l, d_model)
                else:
                    zx = linear(x, w_in[layer], g=norm_mix_g[layer]).reshape(b, l, -1)
                    y, *states = ssd_mixer(zx, layer, n_a, conv0, ssm0, states, *ssd_w, l)
                    x = linear(y.reshape(b * l, d_inner), w_out[layer], residual=x)
            else:
                j = layer - n_a
                lam_init = _lambda_init(layer)
                if past is None:
                    qt = linear(x, w_q[j], g=norm_mix_g[layer], scale=ATT_HEAD_DIM ** -0.5, outs=("bf16_t",),
                                tm=ATT_TILE)
                    o = attention_prompt(qt.reshape(b, l // ATT_TILE, -1, ATT_TILE), k_att, v_att,
                                         lam_vecs[j], att_subln_g[j], slopes, lam_init)
                else:
                    q = linear(x, w_q[j], g=norm_mix_g[layer])
                    o = attention_decode(q.reshape(b, l, -1), k_att, v_att, past[0], past[1], page_table,
                                         lam_vecs[j], att_subln_g[j], slopes, lam_init)
                x = linear(o.reshape(b * l, -1), w_o[j], residual=x)
            i = layer // 2
            if layer % 2 == 0:
                x = ffn(x, norm_ffn_g[layer], f_gate[i], f_up[i], f_down[i])
            else:
                x = moe(x, norm_ffn_g[layer], moe_w_router[i], moe_b_router[i], m_gate, m_up, m_down, i,
                        final_g=norm_final_g if layer == depth - 1 else None)
            if layer == n_a - 1:
                if past is None:
                    assert l % ATT_TILE == 0
                    k_new, k_att = linear(x, w_k, g=norm_kv_g, outs=("f32", "bf16"))
                    v_new, v_att = linear(x, w_v, g=norm_kv_g, outs=("f32", "bf16_t"), tm=ATT_TILE)
                    k_att = k_att.reshape(b, l, -1)
                    v_att = v_att.reshape(b, l // ATT_TILE, -1, ATT_TILE)
                else:
                    k_att = k_new = linear(x, w_k, g=norm_kv_g).reshape(b, l, -1)
                    v_att = v_new = linear(x, w_v, g=norm_kv_g).reshape(b, l, -1)
        y = (x if depth % 2 == 0 else rmsnorm(x, norm_final_g)).reshape(b, l, d_model)
        return (y, states[0].reshape(n_a, b, heads_ssm, SSM_HEAD_DIM, SSM_STATE), states[1],
                k_new.reshape(b, l, att_heads, att_dv), v_new.reshape(b, l, att_heads, att_dv))

    y_p, ssm_p, conv_p, k_p, v_p = trunk(x_prompt, None, None, None)
    bs = x_sample.shape[0]
    y_s, ssm_s, conv_s, k_s, v_s = trunk(x_sample, state_ssm.reshape(n_a, bs, d_inner, SSM_STATE), state_conv,
                                         (cache_k, cache_v))
    return (y_p, y_s, ssm_p, conv_p, k_p, v_p, ssm_s, conv_s, k_s, v_s)
```

```python
import functools
import math

import jax
import jax.numpy as jnp
from jax import lax
from jax.experimental import pallas as pl
from jax.experimental.pallas import tpu as pltpu

F32 = jnp.float32
BF16 = jnp.bfloat16
I32 = jnp.int32

RMS_EPS = 1e-5
SSM_HEAD_DIM = 64
SSM_GROUPS = 8
SSM_STATE = 128
CONV_WIDTH = 4
SSD_CHUNK = 128
ATT_HEAD_DIM = 64
N_EXPERTS = 8
LANE = 128
SUBLANE = 8
VMEM_LIMIT = 56 * 1024 * 1024
NEG = -1e30

TOKEN_TILE = 256
MOE_SLOT_TILE = 256
MOE_CHUNK = 1024
ATT_TILE = 512
PAGES_PER_STEP = 16
DECODE_GROUP = 8
SSD_SEQS_PER_STEP = 4


def _params(*sem):
    return pltpu.CompilerParams(dimension_semantics=sem, vmem_limit_bytes=VMEM_LIMIT)


def _resident(shape):
    nd = len(shape)
    return pl.BlockSpec(shape, lambda *_: (0,) * nd, pipeline_mode=pl.Buffered(1))


def _weight(w):
    if isinstance(w, tuple):
        stack, i = w
        shape = stack.shape[1:]
        zeros = (0,) * len(shape)
        return stack, shape, pl.BlockSpec((None,) + shape, lambda *_: (i,) + zeros,
                                          pipeline_mode=pl.Buffered(1))
    return w, w.shape, _resident(w.shape)


def _rms(x, g):
    ms = jnp.mean(x * x, axis=-1, keepdims=True)
    return x * lax.rsqrt(ms + RMS_EPS) * g


def _silu(x):
    return (0.5 * x) * (1.0 + jnp.tanh(0.5 * x))


def _dot(a, b):
    return jnp.dot(a, b, preferred_element_type=F32)


def _dot_nt(a, b):
    return lax.dot_general(a, b, (((1,), (1,)), ((), ())), preferred_element_type=F32)


def _dot_tn(a, b):
    return lax.dot_general(a, b, (((0,), (0,)), ((), ())), preferred_element_type=F32)


def _split2(x):
    hi = x.astype(BF16)
    lo = (x - hi.astype(F32)).astype(BF16)
    return hi, lo


def _split3(x):
    hi = x.astype(BF16)
    r = x - hi.astype(F32)
    mid = r.astype(BF16)
    lo = (r - mid.astype(F32)).astype(BF16)
    return hi, mid, lo


def _linear_kernel(*refs, norm, residual, scale, outs):
    it = iter(refs)
    x_ref = next(it)
    g_ref = next(it) if norm else None
    w_ref = next(it)
    r_ref = next(it) if residual else None
    x = x_ref[...].astype(F32)
    if norm:
        x = _rms(x, g_ref[...])
    y = _dot(x.astype(BF16), w_ref[...])
    if residual:
        y = y + r_ref[...]
    if scale is not None:
        y = y * scale
    for kind in outs:
        o_ref = next(it)
        if kind == "f32":
            o_ref[...] = y
        elif kind == "bf16":
            o_ref[...] = y.astype(BF16)
        else:
            o_ref[...] = jnp.transpose(y).astype(BF16)


def linear(x, w, g=None, residual=None, scale=None, outs=("f32",), tm=TOKEN_TILE):
    t, k = x.shape
    w, (_, n), w_spec = _weight(w)
    tm = min(tm, t)
    assert t % tm == 0
    args = [x]
    specs = [pl.BlockSpec((tm, k), lambda i: (i, 0))]
    if g is not None:
        args.append(g.reshape(1, k))
        specs.append(_resident((1, k)))
    args.append(w)
    specs.append(w_spec)
    if residual is not None:
        args.append(residual)
        specs.append(pl.BlockSpec((tm, n), lambda i: (i, 0)))
    out_specs, out_shapes = [], []
    for kind in outs:
        if kind == "bf16_t":
            out_specs.append(pl.BlockSpec((None, n, tm), lambda i: (i, 0, 0)))
            out_shapes.append(jax.ShapeDtypeStruct((t // tm, n, tm), BF16))
        else:
            out_specs.append(pl.BlockSpec((tm, n), lambda i: (i, 0)))
            out_shapes.append(jax.ShapeDtypeStruct((t, n), F32 if kind == "f32" else BF16))
    res = pl.pallas_call(
        functools.partial(_linear_kernel, norm=g is not None, residual=residual is not None,
                          scale=scale, outs=tuple(outs)),
        grid=(t // tm,),
        in_specs=specs,
        out_specs=out_specs,
        out_shape=out_shapes,
        compiler_params=_params("parallel"),
        name="linear",
    )(*args)
    return res[0] if len(outs) == 1 else res


def _ffn_kernel(x_ref, g_ref, wg_ref, wu_ref, wd_ref, o_ref):
    x = x_ref[...]
    h = _rms(x, g_ref[...]).astype(BF16)
    a = _dot(h, wg_ref[...])
    u = _dot(h, wu_ref[...])
    act = (_silu(a) * u).astype(BF16)
    o_ref[...] = x + _dot(act, wd_ref[...])


def ffn(x, g, wg, wu, wd, tm=TOKEN_TILE):
    t, d = x.shape
    (wg, _, wg_spec), (wu, _, wu_spec), (wd, _, wd_spec) = _weight(wg), _weight(wu), _weight(wd)
    tm = min(tm, t)
    assert t % tm == 0
    return pl.pallas_call(
        _ffn_kernel,
        grid=(t // tm,),
        in_specs=[pl.BlockSpec((tm, d), lambda i: (i, 0)), _resident((1, d)), wg_spec, wu_spec, wd_spec],
        out_specs=pl.BlockSpec((tm, d), lambda i: (i, 0)),
        out_shape=jax.ShapeDtypeStruct((t, d), F32),
        compiler_params=_params("parallel"),
        name="ffn",
    )(x, g.reshape(1, d), wg, wu, wd)


def _rmsnorm_kernel(x_ref, g_ref, o_ref):
    o_ref[...] = _rms(x_ref[...], g_ref[...])


def rmsnorm(x, g, tm=TOKEN_TILE):
    t, d = x.shape
    tm = min(tm, t)
    assert t % tm == 0
    return pl.pallas_call(
        _rmsnorm_kernel,
        grid=(t // tm,),
        in_specs=[pl.BlockSpec((tm, d), lambda i: (i, 0)), _resident((1, d))],
        out_specs=pl.BlockSpec((tm, d), lambda i: (i, 0)),
        out_shape=jax.ShapeDtypeStruct((t, d), F32),
        compiler_params=_params("parallel"),
        name="rmsnorm",
    )(x, g.reshape(1, d))


def _ssd_kernel(*refs, q, rows, valid, d_inner, heads, fused, has_init, has_prev, single_chunk):
    it = iter(refs)
    src_ref = next(it)
    if fused:
        gmix_ref, win_ref, wout_ref = next(it), next(it), next(it)
    if has_init:
        conv0_ref, ssm0_ref = next(it), next(it)
    cw_ref, cb_ref, dtb_ref, alog_ref, dsk_ref, ng_ref, exp_ref = (next(it) for _ in range(7))
    if has_prev:
        next(it), next(it)
    out_ref, sfin_ref, cnew_ref, xp_ref, stage_ref = (next(it) for _ in range(5))
    ybuf_ref = next(it) if fused else None

    c = pl.program_id(1)
    nc = pl.num_programs(1)
    kp = SSD_CHUNK
    gn = SSM_GROUPS * SSM_STATE
    hpg = heads // SSM_GROUPS
    gw = hpg * SSM_HEAD_DIM
    halo = CONV_WIDTH - 1

    def when(cond):
        return (lambda f: f()) if single_chunk else pl.when(cond)

    @when(c == 0)
    def _():
        xp_ref[0:SUBLANE, :] = jnp.zeros((SUBLANE, xp_ref.shape[1]), F32)
        if has_init:
            sfin_ref[...] = ssm0_ref[...]
            xp_ref[SUBLANE - halo:SUBLANE, :] = conv0_ref[...]
        else:
            sfin_ref[...] = jnp.zeros(sfin_ref.shape, F32)

    if rows < q:
        stage_ref[...] = jnp.zeros(stage_ref.shape, F32)
        stage_ref[0:rows, :] = src_ref[...]
        src = stage_ref[...]
    else:
        src = src_ref[...]
    if fused:
        zx = _dot(_rms(src, gmix_ref[...]).astype(BF16), win_ref[...])
    else:
        zx = src
    z = zx[:, :d_inner]
    xbc = zx[:, d_inner:2 * d_inner + 2 * gn]
    dtr = zx[:, 2 * d_inner + 2 * gn:]

    xp_ref[SUBLANE:SUBLANE + q, :] = xbc
    xall = xp_ref[...]
    conv = cb_ref[...] + xall[SUBLANE:] * cw_ref[halo:halo + 1, :]
    for shift in range(1, CONV_WIDTH):
        k = halo - shift
        conv = conv + pltpu.roll(xall, shift, 0)[SUBLANE:] * cw_ref[k:k + 1, :]
    tail = xp_ref[SUBLANE + valid - halo:SUBLANE + valid, :]
    xp_ref[SUBLANE - halo:SUBLANE, :] = tail

    @when(c == nc - 1)
    def _():
        cnew_ref[...] = tail

    xa = _silu(conv)
    xs = xa[:, :d_inner]
    bm = xa[:, d_inner:d_inner + gn]
    cm = xa[:, d_inner + gn:]

    def pad(v):
        if q == kp:
            return v
        return jnp.concatenate([v, jnp.zeros((kp - q, v.shape[1]), v.dtype)], axis=0)

    row = lax.broadcasted_iota(I32, (q, kp), 0)
    col = lax.broadcasted_iota(I32, (q, kp), 1)
    causal = col <= row

    dtv = dtr + dtb_ref[...]
    dt = jnp.maximum(dtv, 0.0) + jnp.log1p(jnp.exp(-jnp.abs(dtv)))
    if valid < q:
        dt = jnp.where(lax.broadcasted_iota(I32, (q, LANE), 0) < valid, dt, 0.0)
    adt = dt * (-jnp.exp(alog_ref[...]))
    tril = causal.astype(BF16)
    cs = sum(_dot(tril, part) for part in _split3(pad(adt)))
    cs_t = jnp.transpose(pad(cs))
    cs_last = cs[q - 1:q, :]
    e_last = jnp.exp(cs_last)

    stacked = jnp.concatenate([dt, jnp.exp(cs), jnp.exp(cs_last - cs)], axis=0)
    hi, lo = _split2(stacked)
    wide = _dot(hi, exp_ref[...]) + _dot(lo, exp_ref[...])
    dt_w, ecs_w, dout_w = wide[0:q], wide[q:2 * q], wide[2 * q:3 * q]
    xdt = xs * dt_w
    xdtd = pad(xdt * dout_w).astype(BF16)
    xdt_b = pad(xdt).astype(BF16)
    bm_p = pad(bm).astype(BF16)
    cm_b = cm.astype(BF16)
    lane_g = lax.broadcasted_iota(I32, (kp, gw), 1) // SSM_HEAD_DIM
    head_mask = [(lane_g == r).astype(BF16) for r in range(hpg)]

    for g in range(SSM_GROUPS):
        b_g = bm_p[:, g * SSM_STATE:(g + 1) * SSM_STATE]
        c_g = cm_b[:, g * SSM_STATE:(g + 1) * SSM_STATE]
        cb = _dot_nt(c_g, b_g)
        xg = xdt_b[:, g * gw:(g + 1) * gw]
        m_parts, x_parts, scale_parts = [], [], []
        for r in range(hpg):
            h = g * hpg + r
            seg = cs[:, h:h + 1] - cs_t[h:h + 1, :]
            decay = jnp.exp(jnp.where(causal, seg, NEG))
            m_parts.append((cb * decay).astype(BF16))
            x_parts.append(xg * head_mask[r])
            scale_parts.append(jnp.broadcast_to(e_last[:, h:h + 1], (SSM_HEAD_DIM, SSM_STATE)))
        y_diag = _dot(jnp.concatenate(m_parts, axis=1), jnp.concatenate(x_parts, axis=0))
        s_g = sfin_ref[g * gw:(g + 1) * gw, :]
        y_off = _dot_nt(c_g, s_g.astype(BF16)) * ecs_w[:, g * gw:(g + 1) * gw]
        new = _dot_tn(xdtd[:, g * gw:(g + 1) * gw], b_g)
        sfin_ref[g * gw:(g + 1) * gw, :] = s_g * jnp.concatenate(scale_parts, axis=0) + new
        y = y_diag + y_off + dsk_ref[:, g * gw:(g + 1) * gw] * xs[:, g * gw:(g + 1) * gw]
        y = y * _silu(z[:, g * gw:(g + 1) * gw])
        y = y * lax.rsqrt(jnp.mean(y * y, axis=-1, keepdims=True) + RMS_EPS)
        y = y * ng_ref[:, g * gw:(g + 1) * gw]
        if fused:
            ybuf_ref[:, g * gw:(g + 1) * gw] = y.astype(BF16)
        else:
            out_ref[:, g * gw:(g + 1) * gw] = y

    if fused:
        out_ref[...] = src + _dot(ybuf_ref[...], wout_ref[...])


def _ssd_step_kernel(*refs, nb, per_seq, **kw):
    for sb in range(nb):
        _ssd_kernel(*[r.at[sb] if own else r for r, own in zip(refs, per_seq)], **kw)


def ssd_mixer(src, layer, n_layers, conv0_all, ssm0_all, prev, conv_w, conv_b, dt_bias, a_log, d_skip,
              norm_g, valid, fuse=None):
    b, l, width = src.shape
    heads = a_log.shape[0]
    d_inner = heads * SSM_HEAD_DIM
    conv_dim = conv_w.shape[1]
    if l % SSD_CHUNK == 0:
        q = rows = SSD_CHUNK
        nc = l // SSD_CHUNK
        assert valid == SSD_CHUNK
    else:
        assert l < SSD_CHUNK
        rows, nc = l, 1
        q = -(-l // SUBLANE) * SUBLANE
    padh = LANE - heads
    expand = jnp.repeat(jnp.eye(LANE, heads, dtype=BF16), SSM_HEAD_DIM, axis=1)
    halo = CONV_WIDTH - 1

    nb = math.gcd(SSD_SEQS_PER_STEP, b) if nc == 1 else 1
    args = [src]
    specs = [pl.BlockSpec((nb, rows, width), lambda i, j: (i, j, 0))]
    per_seq = [True]
    if fuse is not None:
        g_mix, w_in, w_out = fuse
        (w_in, in_shape, in_spec), (w_out, out_shape, out_spec) = _weight(w_in), _weight(w_out)
        assert in_shape == (width, d_inner + conv_dim + LANE)
        args += [g_mix.reshape(1, width), w_in, w_out]
        specs += [_resident((1, width)), in_spec, out_spec]
        per_seq += [False] * 3
        out_width = out_shape[1]
    else:
        assert width == d_inner + conv_dim + LANE
        out_width = d_inner
    if ssm0_all is not None:
        args += [conv0_all, ssm0_all]
        specs += [pl.BlockSpec((None, nb, halo, conv_dim), lambda i, j: (layer, i, 0, 0)),
                  pl.BlockSpec((None, nb, d_inner, SSM_STATE), lambda i, j: (layer, i, 0, 0))]
        per_seq += [True] * 2
    args += [conv_w, conv_b.reshape(1, conv_dim),
             jnp.pad(dt_bias, (0, padh)).reshape(1, LANE), jnp.pad(a_log, (0, padh)).reshape(1, LANE),
             jnp.repeat(d_skip, SSM_HEAD_DIM).reshape(1, d_inner), norm_g.reshape(1, d_inner), expand]
    specs += [_resident((CONV_WIDTH, conv_dim)), _resident((1, conv_dim)), _resident((1, LANE)),
              _resident((1, LANE)), _resident((1, d_inner)), _resident((1, d_inner)),
              _resident((LANE, d_inner))]
    per_seq += [False] * 7
    aliases = {}
    if prev is not None:
        aliases = {len(args): 1, len(args) + 1: 2}
        args += list(prev)
        specs += [pl.BlockSpec(memory_space=pl.ANY), pl.BlockSpec(memory_space=pl.ANY)]
        per_seq += [False] * 2
    per_seq += [True] * 3
    scratch = [pltpu.VMEM((nb, SUBLANE + q, conv_dim), F32),
               pltpu.VMEM((nb, q, width) if rows < q else (nb, SUBLANE, LANE), F32)]
    per_seq += [True] * 2
    if fuse is not None:
        scratch.append(pltpu.VMEM((q, d_inner), BF16))
        per_seq.append(False)
    kern = functools.partial(_ssd_step_kernel, nb=nb, per_seq=tuple(per_seq), q=q, rows=rows, valid=valid,
                             d_inner=d_inner, heads=heads, fused=fuse is not None,
                             has_init=ssm0_all is not None, has_prev=prev is not None, single_chunk=nc == 1)
    out, sfin, cnew = pl.pallas_call(
        kern,
        grid=(b // nb, nc),
        in_specs=specs,
        out_specs=[
            pl.BlockSpec((nb, q, out_width), lambda i, j: (i, j, 0)),
            pl.BlockSpec((None, nb, d_inner, SSM_STATE), lambda i, j: (layer, i, 0, 0)),
            pl.BlockSpec((None, nb, halo, conv_dim), lambda i, j: (layer, i, 0, 0)),
        ],
        out_shape=[
            jax.ShapeDtypeStruct((b, nc * q, out_width), F32),
            jax.ShapeDtypeStruct((n_layers, b, d_inner, SSM_STATE), F32),
            jax.ShapeDtypeStruct((n_layers, b, halo, conv_dim), F32),
        ],
        scratch_shapes=scratch,
        input_output_aliases=aliases,
        compiler_params=_params("parallel", "arbitrary"),
        name="ssd_mixer",
    )(*args)
    return out[:, :l], sfin, cnew


def _lambda(lam_ref, lam_init):
    lv = lam_ref[...]
    s1 = jnp.sum(lv[0:1] * lv[1:2], axis=-1, keepdims=True)
    s2 = jnp.sum(lv[2:3] * lv[3:4], axis=-1, keepdims=True)
    return jnp.exp(s1) - jnp.exp(s2) + lam_init


def _two_branch_q(qv, n):
    lane = lax.broadcasted_iota(I32, qv.shape, 1)
    zero = jnp.zeros_like(qv)
    return jnp.concatenate([jnp.where(lane < ATT_HEAD_DIM, qv, zero),
                            jnp.where(lane >= ATT_HEAD_DIM, qv, zero)], axis=0).astype(BF16)


def _attn_prompt_kernel(slope_ref, qt_ref, k_ref, vt_ref, lam_ref, sg_ref, o_ref, *, tq, lam_init):
    slope = slope_ref[pl.program_id(1)]
    nq, dv, _ = qt_ref.shape
    feat = lax.broadcasted_iota(I32, (dv, tq), 0)
    cc = lax.broadcasted_iota(I32, (tq, 2 * tq), 1)
    rel = jnp.where(cc >= tq, cc - tq, cc) - lax.broadcasted_iota(I32, (tq, 2 * tq), 0)
    bias = -slope * rel.astype(F32)
    bias_diag = jnp.where(rel >= 0, bias, NEG)
    lam = _lambda(lam_ref, lam_init)
    m0 = jnp.full((1, 2 * tq), NEG, F32)
    l0 = jnp.zeros((1, 2 * tq), F32)
    a0 = jnp.zeros((dv, 2 * tq), F32)

    def q_tile(qi, _):
        qt = qt_ref[qi]
        zero = jnp.zeros_like(qt)
        q2t = jnp.concatenate([jnp.where(feat < ATT_HEAD_DIM, qt, zero),
                               jnp.where(feat >= ATT_HEAD_DIM, qt, zero)], axis=1)

        def scores(j):
            start = pl.multiple_of(j * tq, tq)
            return _dot(k_ref[pl.ds(start, tq), :], q2t)

        def consume(j, s, carry):
            m, l, acc = carry
            tile_bias = -slope * ((qi - j) * tq).astype(F32)
            m_new = jnp.maximum(m, jnp.max(s, axis=0, keepdims=True) + tile_bias)
            alpha = jnp.exp(m - m_new)
            p = jnp.exp(s - (m_new - tile_bias))
            l = alpha * l + jnp.sum(p, axis=0, keepdims=True)
            acc = alpha * acc + _dot(vt_ref[j], p.astype(BF16))
            return m_new, l, acc

        def body(j, c):
            s_next = scores(j + 1)
            return consume(j, c[3] + bias, c[:3]) + (s_next,)

        c = lax.fori_loop(0, qi, body, (m0, l0, a0, scores(0)))
        _, l, acc = consume(qi, c[3] + bias_diag, c[:3])
        o = acc / l
        o = jnp.transpose(o[:, :tq] - lam * o[:, tq:])
        o_ref[pl.ds(pl.multiple_of(qi * tq, tq), tq), :] = _rms(o, sg_ref[...]) * (1.0 - lam_init)
        return 0

    lax.fori_loop(0, nq, q_tile, 0)


def attention_prompt(qt, k, vt, lam_vecs, subln_g, slopes, lam_init):
    b, nq, hd, tq = qt.shape
    l = nq * tq
    dv = subln_g.shape[0]
    heads = hd // dv
    return pl.pallas_call(
        functools.partial(_attn_prompt_kernel, tq=tq, lam_init=lam_init),
        grid_spec=pltpu.PrefetchScalarGridSpec(
            num_scalar_prefetch=1,
            grid=(b, heads),
            in_specs=[
                pl.BlockSpec((None, nq, dv, tq), lambda i, h, s: (i, 0, h, 0)),
                pl.BlockSpec((None, l, dv), lambda i, h, s: (i, 0, h)),
                pl.BlockSpec((None, nq, dv, tq), lambda i, h, s: (i, 0, h, 0)),
                pl.BlockSpec(lam_vecs.shape, lambda i, h, s: (0, 0)),
                pl.BlockSpec((1, dv), lambda i, h, s: (0, 0)),
            ],
            out_specs=pl.BlockSpec((None, l, dv), lambda i, h, s: (i, 0, h)),
        ),
        out_shape=jax.ShapeDtypeStruct((b, l, hd), F32),
        compiler_params=_params("parallel", "parallel"),
        name="attention_prompt",
    )(slopes, qt, k, vt, lam_vecs, subln_g.reshape(1, dv))


def _attn_decode_kernel(pt_ref, slope_ref, q_ref, kn_ref, vn_ref, lam_ref, sg_ref, table_ref, *rest,
                        pp, group, heads, page, n_new, lam_init):
    k_refs = rest[:pp]
    v_refs = rest[pp:2 * pp]
    o_ref = rest[2 * pp]
    m_ref, l_ref, acc_ref = rest[2 * pp + 1:]
    j = pl.program_id(1)
    nj = pl.num_programs(1)
    dv = sg_ref.shape[1]
    nr = 2 * SUBLANE
    pairs = page * heads
    slope_col = jnp.concatenate([jnp.full((nr, 1), slope_ref[h], F32) for h in range(heads)], axis=0)

    @pl.when(j == 0)
    def _():
        m_ref[...] = jnp.full(m_ref.shape, NEG, F32)
        l_ref[...] = jnp.zeros(l_ref.shape, F32)
        acc_ref[...] = jnp.zeros(acc_ref.shape, F32)

    qv = q_ref[...] * (ATT_HEAD_DIM ** -0.5)
    q2 = [_two_branch_q(qv[:, h * dv:(h + 1) * dv], SUBLANE) for h in range(heads)]
    q_all = jnp.concatenate(q2, axis=0)

    def update(s, shift, pv_of):
        m = m_ref[...]
        m_new = jnp.maximum(m, jnp.max(s, axis=-1, keepdims=True) + shift)
        alpha = jnp.exp(m - m_new)
        pr = jnp.exp(s - (m_new - shift))
        l_ref[...] = alpha * l_ref[...] + jnp.sum(pr, axis=-1, keepdims=True)
        acc_ref[...] = alpha * acc_ref[...] + pv_of(pr.astype(BF16))
        m_ref[...] = m_new

    for g0 in range(0, pp, group):
        ks = [k_refs[p][...].astype(BF16) for p in range(g0, g0 + group)]
        vs = [v_refs[p][...].astype(BF16) for p in range(g0, g0 + group)]
        s = jnp.concatenate([_dot_nt(q_all, kb) for kb in ks], axis=1) + table_ref[...]
        start = ((j * pp + g0) * page).astype(F32)
        update(s, slope_col * start,
               lambda pb: sum(_dot(pb[:, i * pairs:(i + 1) * pairs], vs[i]) for i in range(group)))

    @pl.when(j == nj - 1)
    def _():
        row = lax.broadcasted_iota(I32, (heads * nr, page), 0)
        col = lax.broadcasted_iota(I32, (heads * nr, page), 1)
        zeros = jnp.zeros((page - SUBLANE, kn_ref.shape[1]), F32)
        kb = jnp.concatenate([kn_ref[...], zeros], axis=0).astype(BF16)
        vb = jnp.concatenate([vn_ref[...], zeros], axis=0).astype(BF16)
        dist = (row % SUBLANE) - col
        ok = (dist >= 0) & (col < n_new)
        s = jnp.concatenate([_dot_nt(q2[h], kb[:, h * dv:(h + 1) * dv]) for h in range(heads)], axis=0)
        update(s + jnp.where(ok, -slope_col * dist.astype(F32), NEG), jnp.zeros_like(slope_col),
               lambda pb: jnp.concatenate([_dot(pb[h * nr:(h + 1) * nr], vb[:, h * dv:(h + 1) * dv])
                                           for h in range(heads)], axis=0))
        o = acc_ref[...] / l_ref[...]
        lam = _lambda(lam_ref, lam_init)
        for h in range(heads):
            oh = o[h * nr:h * nr + SUBLANE] - lam * o[h * nr + SUBLANE:(h + 1) * nr]
            o_ref[:, h * dv:(h + 1) * dv] = _rms(oh, sg_ref[...]) * (1.0 - lam_init)


def attention_decode(q, k_new, v_new, cache_k, cache_v, page_table, lam_vecs, subln_g, slopes, lam_init):
    b, n_new, hd = q.shape
    dv = subln_g.shape[0]
    heads = hd // dv
    n_pages = page_table.shape[1]
    page = cache_k.shape[1]
    pp = math.gcd(PAGES_PER_STEP, n_pages)
    assert n_new <= SUBLANE
    padr = ((0, 0), (0, SUBLANE - n_new), (0, 0))
    qp, kp, vp = (jnp.pad(a, padr) for a in (q, k_new, v_new))

    n_pool = cache_k.shape[0]
    ck, cv = (c.reshape(n_pool, page * heads, dv) for c in (cache_k, cache_v))

    group = math.gcd(DECODE_GROUP, pp)
    nrow = heads * 2 * SUBLANE
    r = jnp.arange(nrow, dtype=I32)[:, None]
    c = jnp.arange(group * page * heads, dtype=I32)[None, :]
    qpos = (n_pages * page + r % SUBLANE).astype(F32)
    table = jnp.where(r // (2 * SUBLANE) == c % heads,
                      -slopes[r // (2 * SUBLANE)] * (qpos - (c // heads).astype(F32)), NEG)

    def page_spec(p_i):
        return pl.BlockSpec((None, page * heads, dv),
                            lambda i, j, pt, s: (pt[i * n_pages + j * pp + p_i], 0, 0))

    row_spec = pl.BlockSpec((None, SUBLANE, hd), lambda i, j, pt, s: (i, 0, 0))
    out = pl.pallas_call(
        functools.partial(_attn_decode_kernel, pp=pp, group=group, heads=heads, page=page, n_new=n_new,
                          lam_init=lam_init),
        grid_spec=pltpu.PrefetchScalarGridSpec(
            num_scalar_prefetch=2,
            grid=(b, n_pages // pp),
            in_specs=[row_spec, row_spec, row_spec,
                      pl.BlockSpec(lam_vecs.shape, lambda i, j, pt, s: (0, 0)),
                      pl.BlockSpec((1, dv), lambda i, j, pt, s: (0, 0)),
                      pl.BlockSpec(table.shape, lambda i, j, pt, s: (0, 0), pipeline_mode=pl.Buffered(1))]
                     + [page_spec(p_i) for p_i in range(pp)] * 2,
            out_specs=row_spec,
            scratch_shapes=[pltpu.VMEM((heads * 2 * SUBLANE, 1), F32),
                            pltpu.VMEM((heads * 2 * SUBLANE, 1), F32),
                            pltpu.VMEM((heads * 2 * SUBLANE, dv), F32)],
        ),
        out_shape=jax.ShapeDtypeStruct((b, SUBLANE, hd), F32),
        compiler_params=_params("parallel", "arbitrary"),
        name="attention_decode",
    )(page_table.reshape(-1), slopes, qp, kp, vp, lam_vecs, subln_g.reshape(1, dv), table,
      *([ck] * pp), *([cv] * pp))
    return out[:, :n_new]


def _router_kernel(x_ref, g_ref, wr_ref, br_ref, h_ref, idx_ref, gate_ref, cend_ref, run_ref):
    i = pl.program_id(0)
    tm = x_ref.shape[0]

    @pl.when(i == 0)
    def _():
        run_ref[...] = jnp.zeros(run_ref.shape, F32)

    h = _rms(x_ref[...], g_ref[...])
    h_ref[...] = h.astype(BF16)
    logits = lax.dot_general(wr_ref[...], h, (((1,), (1,)), ((), ())), precision=lax.Precision.HIGHEST,
                             preferred_element_type=F32) + br_ref[...]
    ne = logits.shape[0]
    eid = lax.broadcasted_iota(I32, (ne, tm), 0)
    eidf = eid.astype(F32)
    v0 = jnp.max(logits, axis=0, keepdims=True)
    i0 = jnp.min(jnp.where(logits == v0, eidf, float(ne)), axis=0, keepdims=True).astype(I32)
    rest = jnp.where(eid == i0, -jnp.inf, logits)
    v1 = jnp.max(rest, axis=0, keepdims=True)
    i1 = jnp.min(jnp.where(rest == v1, eidf, float(ne)), axis=0, keepdims=True).astype(I32)
    e = jnp.exp(v1 - v0)
    g0 = 1.0 / (1.0 + e)
    g1 = e / (1.0 + e)
    sel0 = eid == i0
    sel1 = eid == i1
    assign = (sel0 | sel1).astype(BF16)
    before = (lax.broadcasted_iota(I32, (tm, tm), 0) < lax.broadcasted_iota(I32, (tm, tm), 1)).astype(BF16)
    rank = _dot(assign, before) + run_ref[...]
    r0 = jnp.sum(jnp.where(sel0, rank, 0.0), axis=0, keepdims=True).astype(I32)
    r1 = jnp.sum(jnp.where(sel1, rank, 0.0), axis=0, keepdims=True).astype(I32)
    idx_ref[...] = jnp.where(eid == 0, i0, jnp.where(eid == 1, i1, jnp.where(eid == 2, r0,
                             jnp.where(eid == 3, r1, 0))))
    gate_ref[...] = jnp.where(eid == 0, g0, jnp.where(eid == 1, g1, 0.0))
    run = run_ref[...] + jnp.sum(assign.astype(F32), axis=1, keepdims=True)
    run_ref[...] = run
    cend_ref[...] = jnp.broadcast_to(run, cend_ref.shape)


def moe_route(x, g, w_router, b_router, tc=MOE_CHUNK):
    t, d = x.shape
    ne = w_router.shape[1]
    assert t % tc == 0 and ne == SUBLANE
    nchunk = t // tc
    return pl.pallas_call(
        _router_kernel,
        grid=(nchunk,),
        in_specs=[pl.BlockSpec((tc, d), lambda i: (i, 0)), _resident((1, d)),
                  _resident((ne, d)), _resident((ne, 1))],
        out_specs=[pl.BlockSpec((tc, d), lambda i: (i, 0)),
                   pl.BlockSpec((ne, tc), lambda i: (0, i)),
                   pl.BlockSpec((ne, tc), lambda i: (0, i)),
                   pl.BlockSpec((None, ne, LANE), lambda i: (i, 0, 0))],
        out_shape=[jax.ShapeDtypeStruct((t, d), BF16), jax.ShapeDtypeStruct((ne, t), I32),
                   jax.ShapeDtypeStruct((ne, t), F32), jax.ShapeDtypeStruct((nchunk, ne, LANE), F32)],
        scratch_shapes=[pltpu.VMEM((ne, 1), F32)],
        compiler_params=_params("arbitrary"),
        name="moe_route",
    )(x, g.reshape(1, d), w_router.T, b_router.reshape(ne, 1))


def _slot_rows(idx_ref, off_ref, slot0, ne):
    idx = idx_ref[...]
    e0, e1, r0, r1 = idx[0:1], idx[1:2], idx[2:3], idx[3:4]
    o0 = jnp.zeros_like(e0)
    o1 = jnp.zeros_like(e1)
    for k in range(ne):
        o0 = jnp.where(e0 == k, off_ref[k], o0)
        o1 = jnp.where(e1 == k, off_ref[k], o1)
    return o0 + r0 - slot0, o1 + r1 - slot0


def _moe_ffn_kernel(wt_ref, wc_ref, wf_ref, te_ref, off_ref, idx_ref, h_ref, wg_ref, wu_ref, wd_ref,
                    o_ref, acc_ref, *, ts, ne):
    w = pl.program_id(0)
    flags = wf_ref[w]

    @pl.when((flags & 2) != 0)
    def _():
        acc_ref[...] = jnp.zeros(acc_ref.shape, F32)

    @pl.when((flags & 1) != 0)
    def _():
        sa, sb = _slot_rows(idx_ref, off_ref, wt_ref[w] * ts, ne)
        rows = lax.broadcasted_iota(I32, (ts, idx_ref.shape[1]), 0)
        pick = ((rows == sa) | (rows == sb)).astype(BF16)
        acc_ref[...] += _dot(pick, h_ref[...])

    @pl.when((flags & 4) != 0)
    def _():
        xb = acc_ref[...].astype(BF16)
        a = _dot(xb, wg_ref[...])
        u = _dot(xb, wu_ref[...])
        o_ref[...] = _dot((_silu(a) * u).astype(BF16), wd_ref[...]).astype(BF16)


def _moe_combine_kernel(wc_ref, wt_ref, wf_ref, off_ref, idx_ref, gate_ref, x_ref, es_ref, *rest,
                        ts, ne, final_norm):
    fg_ref = rest[0] if final_norm else None
    o_ref, acc_ref = rest[-2:]
    w = pl.program_id(0)
    flags = wf_ref[w]

    @pl.when((flags & 2) != 0)
    def _():
        acc_ref[...] = x_ref[...]

    @pl.when((flags & 1) != 0)
    def _():
        sa, sb = _slot_rows(idx_ref, off_ref, wt_ref[w] * ts, ne)
        gt = gate_ref[...]
        rows = lax.broadcasted_iota(I32, (ts, idx_ref.shape[1]), 0)
        wgt = (jnp.where(rows == sa, gt[0:1], 0.0) + jnp.where(rows == sb, gt[1:2], 0.0)).astype(BF16)
        acc_ref[...] += _dot_tn(wgt, es_ref[...])

    @pl.when((flags & 4) != 0)
    def _():
        o_ref[...] = _rms(acc_ref[...], fg_ref[...]) if final_norm else acc_ref[...]


def _work_lists(cend, t, tc, ts, ne):
    nchunk = t // tc
    nt_max = (2 * t) // ts + ne
    w_max = nt_max + ne * nchunk
    cend = cend.astype(I32)
    cstart = jnp.concatenate([jnp.zeros((1, ne), I32), cend[:-1]], axis=0)
    cnt = cend[-1]
    tiles = (cnt + ts - 1) // ts
    tile_end = jnp.cumsum(tiles)
    off = (tile_end - tiles) * ts
    n_tiles = tile_end[-1]
    tile_ids = jnp.arange(nt_max, dtype=I32)

    def count_le(sorted_vals, queries):
        return jnp.sum(sorted_vals[None, :] <= queries[:, None], axis=1).astype(I32)

    te = jnp.minimum(count_le(tile_end, tile_ids), ne - 1)

    def flatten(counts, w_total):
        ends = jnp.cumsum(counts)
        total = ends[-1]
        wi = jnp.clip(jnp.arange(w_total, dtype=I32), 0, jnp.maximum(total - 1, 0))
        owner = jnp.minimum(count_le(ends, wi), counts.shape[0] - 1)
        local = wi - (ends[owner] - counts[owner])
        valid = jnp.arange(w_total, dtype=I32) < total
        return owner, local, valid

    k0 = tile_ids * ts - off[te]
    k1 = jnp.minimum(k0 + ts, cnt[te])
    ce_t = cend[:, te]
    cs_t = cstart[:, te]
    c_lo = jnp.sum(ce_t <= k0[None, :], axis=0).astype(I32)
    c_hi = (nchunk - 1 - jnp.sum(cs_t >= k1[None, :], axis=0)).astype(I32)
    active = tile_ids < n_tiles
    n_items = jnp.where(active, c_hi - c_lo + 1, 0)
    owner, local, valid = flatten(n_items, w_max)
    f_tile, f_chunk = owner, jnp.clip(c_lo[owner] + local, 0, nchunk - 1)
    first = local == 0
    last = local == n_items[owner] - 1
    f_flags = (valid * (1 + 2 * first + 4 * last)).astype(I32)

    has = cend > cstart
    s_lo = (off[None, :] + cstart) // ts
    s_hi = (off[None, :] + cend - 1) // ts
    n_ce = jnp.where(has, s_hi - s_lo + 1, 0).reshape(-1)
    owner, local, valid = flatten(n_ce, w_max)
    c_chunk = owner // ne
    c_tile = jnp.clip(s_lo.reshape(-1)[owner] + local, 0, nt_max - 1)
    per_chunk = jnp.sum(n_ce.reshape(nchunk, ne), axis=1)
    chunk_end = jnp.cumsum(per_chunk)
    wi = jnp.clip(jnp.arange(w_max, dtype=I32), 0, jnp.maximum(chunk_end[-1] - 1, 0))
    first = wi == (chunk_end - per_chunk)[c_chunk]
    last = wi == chunk_end[c_chunk] - 1
    c_flags = (valid * (1 + 2 * first + 4 * last)).astype(I32)
    return (f_tile.astype(I32), f_chunk.astype(I32), f_flags, te, off.astype(I32),
            c_chunk.astype(I32), c_tile.astype(I32), c_flags, nt_max, w_max)


def moe(x, g, w_router, b_router, wg, wu, wd, layer, final_g=None, tc=MOE_CHUNK, ts=MOE_SLOT_TILE):
    t, d = x.shape
    _, ne, _, f = wg.shape
    tc = min(tc, t)
    h, idx, gate, cend = moe_route(x, g, w_router, b_router, tc)
    (f_tile, f_chunk, f_flags, te, off, c_chunk, c_tile, c_flags, nt_max, w_max) = _work_lists(
        cend[:, :, 0], t, tc, ts, ne)

    sorted_out = pl.pallas_call(
        functools.partial(_moe_ffn_kernel, ts=ts, ne=ne),
        grid_spec=pltpu.PrefetchScalarGridSpec(
            num_scalar_prefetch=5,
            grid=(w_max,),
            in_specs=[
                pl.BlockSpec((ne, tc), lambda w, wt, wc, wf, te_, of: (0, wc[w])),
                pl.BlockSpec((tc, d), lambda w, wt, wc, wf, te_, of: (wc[w], 0)),
                pl.BlockSpec((None, None, d, f), lambda w, wt, wc, wf, te_, of: (layer, te_[wt[w]], 0, 0)),
                pl.BlockSpec((None, None, d, f), lambda w, wt, wc, wf, te_, of: (layer, te_[wt[w]], 0, 0)),
                pl.BlockSpec((None, None, f, d), lambda w, wt, wc, wf, te_, of: (layer, te_[wt[w]], 0, 0)),
            ],
            out_specs=pl.BlockSpec((ts, d), lambda w, wt, wc, wf, te_, of: (wt[w], 0)),
            scratch_shapes=[pltpu.VMEM((ts, d), F32)],
        ),
        out_shape=jax.ShapeDtypeStruct((nt_max * ts, d), BF16),
        compiler_params=_params("arbitrary"),
        name="moe_ffn",
    )(f_tile, f_chunk, f_flags, te, off, idx, h, wg, wu, wd)

    return pl.pallas_call(
        functools.partial(_moe_combine_kernel, ts=ts, ne=ne, final_norm=final_g is not None),
        grid_spec=pltpu.PrefetchScalarGridSpec(
            num_scalar_prefetch=4,
            grid=(w_max,),
            in_specs=[
                pl.BlockSpec((ne, tc), lambda w, wc, wt, wf, of: (0, wc[w])),
                pl.BlockSpec((ne, tc), lambda w, wc, wt, wf, of: (0, wc[w])),
                pl.BlockSpec((tc, d), lambda w, wc, wt, wf, of: (wc[w], 0)),
                pl.BlockSpec((ts, d), lambda w, wc, wt, wf, of: (wt[w], 0)),
            ] + ([pl.BlockSpec((1, d), lambda w, wc, wt, wf, of: (0, 0))] if final_g is not None else []),
            out_specs=pl.BlockSpec((tc, d), lambda w, wc, wt, wf, of: (wc[w], 0)),
            scratch_shapes=[pltpu.VMEM((tc, d), F32)],
        ),
        out_shape=jax.ShapeDtypeStruct((t, d), F32),
        compiler_params=_params("arbitrary"),
        name="moe_combine",
    )(c_chunk, c_tile, c_flags, off, idx, gate, x, sorted_out,
      *([final_g.reshape(1, d)] if final_g is not None else []))


def _lambda_init(layer):
    return 0.8 - 0.6 * math.exp(-0.3 * layer)


def kernel(x_prompt, x_sample, state_ssm, state_conv, cache_k, cache_v, page_table, norm_mix_g, norm_ffn_g, norm_kv_g, norm_final_g, ssm_w_in, ssm_conv_w, ssm_conv_b, ssm_dt_bias, ssm_a_log, ssm_d, ssm_norm_g, ssm_w_out, kv_w_k, kv_w_v, att_w_q, att_lam_q1, att_lam_k1, att_lam_q2, att_lam_k2, att_subln_g, att_w_o, ffn_w_gate, ffn_w_up, ffn_w_down, moe_w_router, moe_b_router, moe_w_gate, moe_w_up, moe_w_down):
    depth, d_model = norm_mix_g.shape
    n_a = ssm_w_in.shape[0]
    heads_ssm = ssm_a_log.shape[1]
    d_inner = heads_ssm * SSM_HEAD_DIM
    conv_dim = ssm_conv_w.shape[2]
    att_dv = att_subln_g.shape[1]
    att_heads = kv_w_v.shape[1] // att_dv
    slopes = 2.0 ** (-8.0 * jnp.arange(1, att_heads + 1, dtype=F32) / att_heads)

    def per_layer(w):
        stack = w.astype(BF16)
        return [(stack, i) for i in range(w.shape[0])]

    in_dim = ssm_w_in.shape[2]
    w_in = per_layer(jnp.pad(ssm_w_in, ((0, 0), (0, 0), (0, d_inner + conv_dim + LANE - in_dim))))
    w_out = per_layer(ssm_w_out)
    w_k, w_v = kv_w_k.astype(BF16), kv_w_v.astype(BF16)
    w_q, w_o = per_layer(att_w_q), per_layer(att_w_o)
    f_gate, f_up, f_down = per_layer(ffn_w_gate), per_layer(ffn_w_up), per_layer(ffn_w_down)
    m_gate, m_up, m_down = moe_w_gate.astype(BF16), moe_w_up.astype(BF16), moe_w_down.astype(BF16)
    lam_vecs = jnp.stack([att_lam_q1, att_lam_k1, att_lam_q2, att_lam_k2], axis=1)

    def trunk(x3, ssm0, conv0, past):
        b, l, _ = x3.shape
        x = x3.reshape(b * l, d_model)
        states = None
        k_new = v_new = k_att = v_att = None
        for layer in range(depth):
            if layer < n_a:
                ssd_w = (ssm_conv_w[layer], ssm_conv_b[layer], ssm_dt_bias[layer], ssm_a_log[layer],
                         ssm_d[layer], ssm_norm_g[layer])
                if l % SSD_CHUNK == 0:
                    xo, *states = ssd_mixer(x.reshape(b, l, d_model), layer, n_a, conv0, ssm0, states, *ssd_w,
                                            SSD_CHUNK, fuse=(norm_mix_g[layer], w_in[layer], w_out[layer]))
                    x = xo.reshape(b * l, d_model)
                else:
                    zx = linear(x, w_in[layer], g=norm_mix_g[layer]).reshape(b, l, -1)
                    y, *states = ssd_mixer(zx, layer, n_a, conv0, ssm0, states, *ssd_w, l)
                    x = linear(y.reshape(b * l, d_inner), w_out[layer], residual=x)
            else:
                j = layer - n_a
                lam_init = _lambda_init(layer)
                if past is None:
                    qt = linear(x, w_q[j], g=norm_mix_g[layer], scale=ATT_HEAD_DIM ** -0.5, outs=("bf16_t",),
                                tm=ATT_TILE)
                    o = attention_prompt(qt.reshape(b, l // ATT_TILE, -1, ATT_TILE), k_att, v_att,
                                         lam_vecs[j], att_subln_g[j], slopes, lam_init)
                else:
                    q = linear(x, w_q[j], g=norm_mix_g[layer])
                    o = attention_decode(q.reshape(b, l, -1), k_att, v_att, past[0], past[1], page_table,
                                         lam_vecs[j], att_subln_g[j], slopes, lam_init)
                x = linear(o.reshape(b * l, -1), w_o[j], residual=x)
            i = layer // 2
            if layer % 2 == 0:
                x = ffn(x, norm_ffn_g[layer], f_gate[i], f_up[i], f_down[i])
            else:
                x = moe(x, norm_ffn_g[layer], moe_w_router[i], moe_b_router[i], m_gate, m_up, m_down, i,
                        final_g=norm_final_g if layer == depth - 1 else None)
            if layer == n_a - 1:
                if past is None:
                    assert l % ATT_TILE == 0
                    k_new, k_att = linear(x, w_k, g=norm_kv_g, outs=("f32", "bf16"))
                    v_new, v_att = linear(x, w_v, g=norm_kv_g, outs=("f32", "bf16_t"), tm=ATT_TILE)
                    k_att = k_att.reshape(b, l, -1)
                    v_att = v_att.reshape(b, l // ATT_TILE, -1, ATT_TILE)
                else:
                    k_att = k_new = linear(x, w_k, g=norm_kv_g).reshape(b, l, -1)
                    v_att = v_new = linear(x, w_v, g=norm_kv_g).reshape(b, l, -1)
        y = (x if depth % 2 == 0 else rmsnorm(x, norm_final_g)).reshape(b, l, d_model)
        return (y, states[0].reshape(n_a, b, heads_ssm, SSM_HEAD_DIM, SSM_STATE), states[1],
                k_new.reshape(b, l, att_heads, att_dv), v_new.reshape(b, l, att_heads, att_dv))

    y_p, ssm_p, conv_p, k_p, v_p = trunk(x_prompt, None, None, None)
    bs = x_sample.shape[0]
    y_s, ssm_s, conv_s, k_s, v_s = trunk(x_sample, state_ssm.reshape(n_a, bs, d_inner, SSM_STATE), state_conv,
                                         (cache_k, cache_v))
    return (y_p, y_s, ssm_p, conv_p, k_p, v_p, ssm_s, conv_s, k_s, v_s)
```

```python
import functools
import math

import jax
import jax.numpy as jnp
from jax import lax
from jax.experimental import pallas as pl
from jax.experimental.pallas import tpu as pltpu

F32 = jnp.float32
BF16 = jnp.bfloat16
I32 = jnp.int32

RMS_EPS = 1e-5
SSM_HEAD_DIM = 64
SSM_GROUPS = 8
SSM_STATE = 128
CONV_WIDTH = 4
SSD_CHUNK = 128
ATT_HEAD_DIM = 64
N_EXPERTS = 8
LANE = 128
SUBLANE = 8
VMEM_LIMIT = 56 * 1024 * 1024
NEG = -1e30

TOKEN_TILE = 256
MOE_SLOT_TILE = 256
MOE_CHUNK = 1024
ATT_TILE = 512
PAGES_PER_STEP = 16
DECODE_GROUP = 16
SSD_SEQS_PER_STEP = 4


def _params(*sem):
    return pltpu.CompilerParams(dimension_semantics=sem, vmem_limit_bytes=VMEM_LIMIT)


def _resident(shape):
    nd = len(shape)
    return pl.BlockSpec(shape, lambda *_: (0,) * nd, pipeline_mode=pl.Buffered(1))


def _weight(w):
    if isinstance(w, tuple):
        stack, i = w
        shape = stack.shape[1:]
        zeros = (0,) * len(shape)
        return stack, shape, pl.BlockSpec((None,) + shape, lambda *_: (i,) + zeros,
                                          pipeline_mode=pl.Buffered(1))
    return w, w.shape, _resident(w.shape)


def _rms(x, g):
    ms = jnp.mean(x * x, axis=-1, keepdims=True)
    return x * lax.rsqrt(ms + RMS_EPS) * g


def _silu(x):
    return (0.5 * x) * (1.0 + jnp.tanh(0.5 * x))


def _dot(a, b):
    return jnp.dot(a, b, preferred_element_type=F32)


def _dot_nt(a, b):
    return lax.dot_general(a, b, (((1,), (1,)), ((), ())), preferred_element_type=F32)


def _dot_tn(a, b):
    return lax.dot_general(a, b, (((0,), (0,)), ((), ())), preferred_element_type=F32)


def _split2(x):
    hi = x.astype(BF16)
    lo = (x - hi.astype(F32)).astype(BF16)
    return hi, lo


def _split3(x):
    hi = x.astype(BF16)
    r = x - hi.astype(F32)
    mid = r.astype(BF16)
    lo = (r - mid.astype(F32)).astype(BF16)
    return hi, mid, lo


def _linear_kernel(*refs, norm, residual, scale, outs):
    it = iter(refs)
    x_ref = next(it)
    g_ref = next(it) if norm else None
    w_ref = next(it)
    r_ref = next(it) if residual else None
    x = x_ref[...].astype(F32)
    if norm:
        x = _rms(x, g_ref[...])
    y = _dot(x.astype(BF16), w_ref[...])
    if residual:
        y = y + r_ref[...]
    if scale is not None:
        y = y * scale
    for kind in outs:
        o_ref = next(it)
        if kind == "f32":
            o_ref[...] = y
        elif kind == "bf16":
            o_ref[...] = y.astype(BF16)
        else:
            o_ref[...] = jnp.transpose(y).astype(BF16)


def linear(x, w, g=None, residual=None, scale=None, outs=("f32",), tm=TOKEN_TILE):
    t, k = x.shape
    w, (_, n), w_spec = _weight(w)
    tm = min(tm, t)
    assert t % tm == 0
    args = [x]
    specs = [pl.BlockSpec((tm, k), lambda i: (i, 0))]
    if g is not None:
        args.append(g.reshape(1, k))
        specs.append(_resident((1, k)))
    args.append(w)
    specs.append(w_spec)
    if residual is not None:
        args.append(residual)
        specs.append(pl.BlockSpec((tm, n), lambda i: (i, 0)))
    out_specs, out_shapes = [], []
    for kind in outs:
        if kind == "bf16_t":
            out_specs.append(pl.BlockSpec((None, n, tm), lambda i: (i, 0, 0)))
            out_shapes.append(jax.ShapeDtypeStruct((t // tm, n, tm), BF16))
        else:
            out_specs.append(pl.BlockSpec((tm, n), lambda i: (i, 0)))
            out_shapes.append(jax.ShapeDtypeStruct((t, n), F32 if kind == "f32" else BF16))
    res = pl.pallas_call(
        functools.partial(_linear_kernel, norm=g is not None, residual=residual is not None,
                          scale=scale, outs=tuple(outs)),
        grid=(t // tm,),
        in_specs=specs,
        out_specs=out_specs,
        out_shape=out_shapes,
        compiler_params=_params("parallel"),
        name="linear",
    )(*args)
    return res[0] if len(outs) == 1 else res


def _ffn_kernel(x_ref, g_ref, wg_ref, wu_ref, wd_ref, o_ref):
    x = x_ref[...]
    h = _rms(x, g_ref[...]).astype(BF16)
    a = _dot(h, wg_ref[...])
    u = _dot(h, wu_ref[...])
    act = (_silu(a) * u).astype(BF16)
    o_ref[...] = x + _dot(act, wd_ref[...])


def ffn(x, g, wg, wu, wd, tm=TOKEN_TILE):
    t, d = x.shape
    (wg, _, wg_spec), (wu, _, wu_spec), (wd, _, wd_spec) = _weight(wg), _weight(wu), _weight(wd)
    tm = min(tm, t)
    assert t % tm == 0
    return pl.pallas_call(
        _ffn_kernel,
        grid=(t // tm,),
        in_specs=[pl.BlockSpec((tm, d), lambda i: (i, 0)), _resident((1, d)), wg_spec, wu_spec, wd_spec],
        out_specs=pl.BlockSpec((tm, d), lambda i: (i, 0)),
        out_shape=jax.ShapeDtypeStruct((t, d), F32),
        compiler_params=_params("parallel"),
        name="ffn",
    )(x, g.reshape(1, d), wg, wu, wd)


def _rmsnorm_kernel(x_ref, g_ref, o_ref):
    o_ref[...] = _rms(x_ref[...], g_ref[...])


def rmsnorm(x, g, tm=TOKEN_TILE):
    t, d = x.shape
    tm = min(tm, t)
    assert t % tm == 0
    return pl.pallas_call(
        _rmsnorm_kernel,
        grid=(t // tm,),
        in_specs=[pl.BlockSpec((tm, d), lambda i: (i, 0)), _resident((1, d))],
        out_specs=pl.BlockSpec((tm, d), lambda i: (i, 0)),
        out_shape=jax.ShapeDtypeStruct((t, d), F32),
        compiler_params=_params("parallel"),
        name="rmsnorm",
    )(x, g.reshape(1, d))


def _ssd_kernel(*refs, q, rows, valid, d_inner, heads, fused, has_init, has_prev, single_chunk):
    it = iter(refs)
    src_ref = next(it)
    if fused:
        gmix_ref, win_ref, wout_ref = next(it), next(it), next(it)
    if has_init:
        conv0_ref, ssm0_ref = next(it), next(it)
    cw_ref, cb_ref, dtb_ref, alog_ref, dsk_ref, ng_ref, exp_ref = (next(it) for _ in range(7))
    if has_prev:
        next(it), next(it)
    out_ref, sfin_ref, cnew_ref, xp_ref, stage_ref = (next(it) for _ in range(5))
    ybuf_ref = next(it) if fused else None

    c = pl.program_id(1)
    nc = pl.num_programs(1)
    kp = SSD_CHUNK
    gn = SSM_GROUPS * SSM_STATE
    hpg = heads // SSM_GROUPS
    gw = hpg * SSM_HEAD_DIM
    halo = CONV_WIDTH - 1

    def when(cond):
        return (lambda f: f()) if single_chunk else pl.when(cond)

    @when(c == 0)
    def _():
        xp_ref[0:SUBLANE, :] = jnp.zeros((SUBLANE, xp_ref.shape[1]), F32)
        if has_init:
            sfin_ref[...] = ssm0_ref[...]
            xp_ref[SUBLANE - halo:SUBLANE, :] = conv0_ref[...]
        else:
            sfin_ref[...] = jnp.zeros(sfin_ref.shape, F32)

    if rows < q:
        stage_ref[...] = jnp.zeros(stage_ref.shape, F32)
        stage_ref[0:rows, :] = src_ref[...]
        src = stage_ref[...]
    else:
        src = src_ref[...]
    if fused:
        zx = _dot(_rms(src, gmix_ref[...]).astype(BF16), win_ref[...])
    else:
        zx = src
    z = zx[:, :d_inner]
    xbc = zx[:, d_inner:2 * d_inner + 2 * gn]
    dtr = zx[:, 2 * d_inner + 2 * gn:]

    xp_ref[SUBLANE:SUBLANE + q, :] = xbc
    xall = xp_ref[...]
    conv = cb_ref[...] + xall[SUBLANE:] * cw_ref[halo:halo + 1, :]
    for shift in range(1, CONV_WIDTH):
        k = halo - shift
        conv = conv + pltpu.roll(xall, shift, 0)[SUBLANE:] * cw_ref[k:k + 1, :]
    tail = xp_ref[SUBLANE + valid - halo:SUBLANE + valid, :]
    xp_ref[SUBLANE - halo:SUBLANE, :] = tail

    @when(c == nc - 1)
    def _():
        cnew_ref[...] = tail

    xa = _silu(conv)
    xs = xa[:, :d_inner]
    bm = xa[:, d_inner:d_inner + gn]
    cm = xa[:, d_inner + gn:]

    def pad(v):
        if q == kp:
            return v
        return jnp.concatenate([v, jnp.zeros((kp - q, v.shape[1]), v.dtype)], axis=0)

    row = lax.broadcasted_iota(I32, (q, kp), 0)
    col = lax.broadcasted_iota(I32, (q, kp), 1)
    causal = col <= row

    dtv = dtr + dtb_ref[...]
    dt = jnp.maximum(dtv, 0.0) + jnp.log1p(jnp.exp(-jnp.abs(dtv)))
    if valid < q:
        dt = jnp.where(lax.broadcasted_iota(I32, (q, LANE), 0) < valid, dt, 0.0)
    adt = dt * (-jnp.exp(alog_ref[...]))
    tril = causal.astype(BF16)
    cs = sum(_dot(tril, part) for part in _split3(pad(adt)))
    cs_t = jnp.transpose(pad(cs))
    cs_last = cs[q - 1:q, :]
    e_last = jnp.exp(cs_last)

    stacked = jnp.concatenate([dt, jnp.exp(cs), jnp.exp(cs_last - cs)], axis=0)
    hi, lo = _split2(stacked)
    wide = _dot(hi, exp_ref[...]) + _dot(lo, exp_ref[...])
    dt_w, ecs_w, dout_w = wide[0:q], wide[q:2 * q], wide[2 * q:3 * q]
    xdt = xs * dt_w
    xdtd = pad(xdt * dout_w).astype(BF16)
    xdt_b = pad(xdt).astype(BF16)
    bm_p = pad(bm).astype(BF16)
    cm_b = cm.astype(BF16)
    lane_g = lax.broadcasted_iota(I32, (kp, gw), 1) // SSM_HEAD_DIM
    head_mask = [(lane_g == r).astype(BF16) for r in range(hpg)]

    for g in range(SSM_GROUPS):
        b_g = bm_p[:, g * SSM_STATE:(g + 1) * SSM_STATE]
        c_g = cm_b[:, g * SSM_STATE:(g + 1) * SSM_STATE]
        cb = _dot_nt(c_g, b_g)
        xg = xdt_b[:, g * gw:(g + 1) * gw]
        m_parts, x_parts, scale_parts = [], [], []
        for r in range(hpg):
            h = g * hpg + r
            seg = cs[:, h:h + 1] - cs_t[h:h + 1, :]
            decay = jnp.exp(jnp.where(causal, seg, NEG))
            m_parts.append((cb * decay).astype(BF16))
            x_parts.append(xg * head_mask[r])
            scale_parts.append(jnp.broadcast_to(e_last[:, h:h + 1], (SSM_HEAD_DIM, SSM_STATE)))
        y_diag = _dot(jnp.concatenate(m_parts, axis=1), jnp.concatenate(x_parts, axis=0))
        s_g = sfin_ref[g * gw:(g + 1) * gw, :]
        y_off = _dot_nt(c_g, s_g.astype(BF16)) * ecs_w[:, g * gw:(g + 1) * gw]
        new = _dot_tn(xdtd[:, g * gw:(g + 1) * gw], b_g)
        sfin_ref[g * gw:(g + 1) * gw, :] = s_g * jnp.concatenate(scale_parts, axis=0) + new
        y = y_diag + y_off + dsk_ref[:, g * gw:(g + 1) * gw] * xs[:, g * gw:(g + 1) * gw]
        y = y * _silu(z[:, g * gw:(g + 1) * gw])
        y = y * lax.rsqrt(jnp.mean(y * y, axis=-1, keepdims=True) + RMS_EPS)
        y = y * ng_ref[:, g * gw:(g + 1) * gw]
        if fused:
            ybuf_ref[:, g * gw:(g + 1) * gw] = y.astype(BF16)
        else:
            out_ref[:, g * gw:(g + 1) * gw] = y

    if fused:
        out_ref[...] = src + _dot(ybuf_ref[...], wout_ref[...])


def _ssd_step_kernel(*refs, nb, per_seq, **kw):
    for sb in range(nb):
        _ssd_kernel(*[r.at[sb] if own else r for r, own in zip(refs, per_seq)], **kw)


def ssd_mixer(src, layer, n_layers, conv0_all, ssm0_all, prev, conv_w, conv_b, dt_bias, a_log, d_skip,
              norm_g, valid, fuse=None):
    b, l, width = src.shape
    heads = a_log.shape[0]
    d_inner = heads * SSM_HEAD_DIM
    conv_dim = conv_w.shape[1]
    if l % SSD_CHUNK == 0:
        q = rows = SSD_CHUNK
        nc = l // SSD_CHUNK
        assert valid == SSD_CHUNK
    else:
        assert l < SSD_CHUNK
        rows, nc = l, 1
        q = -(-l // SUBLANE) * SUBLANE
    padh = LANE - heads
    expand = jnp.repeat(jnp.eye(LANE, heads, dtype=BF16), SSM_HEAD_DIM, axis=1)
    halo = CONV_WIDTH - 1

    nb = math.gcd(SSD_SEQS_PER_STEP, b) if nc == 1 else 1
    args = [src]
    specs = [pl.BlockSpec((nb, rows, width), lambda i, j: (i, j, 0))]
    per_seq = [True]
    if fuse is not None:
        g_mix, w_in, w_out = fuse
        (w_in, in_shape, in_spec), (w_out, out_shape, out_spec) = _weight(w_in), _weight(w_out)
        assert in_shape == (width, d_inner + conv_dim + LANE)
        args += [g_mix.reshape(1, width), w_in, w_out]
        specs += [_resident((1, width)), in_spec, out_spec]
        per_seq += [False] * 3
        out_width = out_shape[1]
    else:
        assert width == d_inner + conv_dim + LANE
        out_width = d_inner
    if ssm0_all is not None:
        args += [conv0_all, ssm0_all]
        specs += [pl.BlockSpec((None, nb, halo, conv_dim), lambda i, j: (layer, i, 0, 0)),
                  pl.BlockSpec((None, nb, d_inner, SSM_STATE), lambda i, j: (layer, i, 0, 0))]
        per_seq += [True] * 2
    args += [conv_w, conv_b.reshape(1, conv_dim),
             jnp.pad(dt_bias, (0, padh)).reshape(1, LANE), jnp.pad(a_log, (0, padh)).reshape(1, LANE),
             jnp.repeat(d_skip, SSM_HEAD_DIM).reshape(1, d_inner), norm_g.reshape(1, d_inner), expand]
    specs += [_resident((CONV_WIDTH, conv_dim)), _resident((1, conv_dim)), _resident((1, LANE)),
              _resident((1, LANE)), _resident((1, d_inner)), _resident((1, d_inner)),
              _resident((LANE, d_inner))]
    per_seq += [False] * 7
    aliases = {}
    if prev is not None:
        aliases = {len(args): 1, len(args) + 1: 2}
        args += list(prev)
        specs += [pl.BlockSpec(memory_space=pl.ANY), pl.BlockSpec(memory_space=pl.ANY)]
        per_seq += [False] * 2
    per_seq += [True] * 3
    scratch = [pltpu.VMEM((nb, SUBLANE + q, conv_dim), F32),
               pltpu.VMEM((nb, q, width) if rows < q else (nb, SUBLANE, LANE), F32)]
    per_seq += [True] * 2
    if fuse is not None:
        scratch.append(pltpu.VMEM((q, d_inner), BF16))
        per_seq.append(False)
    kern = functools.partial(_ssd_step_kernel, nb=nb, per_seq=tuple(per_seq), q=q, rows=rows, valid=valid,
                             d_inner=d_inner, heads=heads, fused=fuse is not None,
                             has_init=ssm0_all is not None, has_prev=prev is not None, single_chunk=nc == 1)
    out, sfin, cnew = pl.pallas_call(
        kern,
        grid=(b // nb, nc),
        in_specs=specs,
        out_specs=[
            pl.BlockSpec((nb, q, out_width), lambda i, j: (i, j, 0)),
            pl.BlockSpec((None, nb, d_inner, SSM_STATE), lambda i, j: (layer, i, 0, 0)),
            pl.BlockSpec((None, nb, halo, conv_dim), lambda i, j: (layer, i, 0, 0)),
        ],
        out_shape=[
            jax.ShapeDtypeStruct((b, nc * q, out_width), F32),
            jax.ShapeDtypeStruct((n_layers, b, d_inner, SSM_STATE), F32),
            jax.ShapeDtypeStruct((n_layers, b, halo, conv_dim), F32),
        ],
        scratch_shapes=scratch,
        input_output_aliases=aliases,
        compiler_params=_params("parallel", "arbitrary"),
        name="ssd_mixer",
    )(*args)
    return out[:, :l], sfin, cnew


def _lambda(lam_ref, lam_init):
    lv = lam_ref[...]
    s1 = jnp.sum(lv[0:1] * lv[1:2], axis=-1, keepdims=True)
    s2 = jnp.sum(lv[2:3] * lv[3:4], axis=-1, keepdims=True)
    return jnp.exp(s1) - jnp.exp(s2) + lam_init


def _two_branch_q(qv, n):
    lane = lax.broadcasted_iota(I32, qv.shape, 1)
    zero = jnp.zeros_like(qv)
    return jnp.concatenate([jnp.where(lane < ATT_HEAD_DIM, qv, zero),
                            jnp.where(lane >= ATT_HEAD_DIM, qv, zero)], axis=0).astype(BF16)


def _attn_prompt_kernel(slope_ref, qt_ref, k_ref, vt_ref, lam_ref, sg_ref, o_ref, *, tq, lam_init):
    slope = slope_ref[pl.program_id(1)]
    nq, dv, _ = qt_ref.shape
    feat = lax.broadcasted_iota(I32, (dv, tq), 0)
    cc = lax.broadcasted_iota(I32, (tq, 2 * tq), 1)
    rel = jnp.where(cc >= tq, cc - tq, cc) - lax.broadcasted_iota(I32, (tq, 2 * tq), 0)
    bias = -slope * rel.astype(F32)
    bias_diag = jnp.where(rel >= 0, bias, NEG)
    lam = _lambda(lam_ref, lam_init)
    m0 = jnp.full((1, 2 * tq), NEG, F32)
    l0 = jnp.zeros((1, 2 * tq), F32)
    a0 = jnp.zeros((dv, 2 * tq), F32)

    def q_tile(qi, _):
        qt = qt_ref[qi]
        zero = jnp.zeros_like(qt)
        q2t = jnp.concatenate([jnp.where(feat < ATT_HEAD_DIM, qt, zero),
                               jnp.where(feat >= ATT_HEAD_DIM, qt, zero)], axis=1)

        def scores(j):
            start = pl.multiple_of(j * tq, tq)
            return _dot(k_ref[pl.ds(start, tq), :], q2t)

        def update(s, vt, carry, tile_bias):
            m, l, acc = carry
            m_new = jnp.maximum(m, jnp.max(s, axis=0, keepdims=True) + tile_bias)
            alpha = jnp.exp(m - m_new)
            p = jnp.exp(s - (m_new - tile_bias))
            l = alpha * l + jnp.sum(p, axis=0, keepdims=True)
            acc = alpha * acc + _dot(vt, p.astype(BF16))
            return m_new, l, acc

        def body(j, c):
            s_next = scores(j + 1)
            tile_bias = -slope * ((qi - j) * tq).astype(F32)
            return update(c[3] + bias, vt_ref[j], c[:3], tile_bias) + (s_next,)

        c = lax.fori_loop(0, qi, body, (m0, l0, a0, scores(0)))
        half = tq // 2
        raw, vt_d = c[3], vt_ref[qi]
        carry = update(raw[:half] + bias_diag[:half], vt_d[:, :half], c[:3], 0.0)

        def late(a):
            return jnp.concatenate([a[..., half:tq], a[..., tq + half:]], axis=-1)

        part = update(late(raw[half:]) + late(bias_diag[half:]), vt_d[:, half:],
                      tuple(late(a) for a in carry), 0.0)
        _, l, acc = (jnp.concatenate([f[..., :half], p_[..., :half], f[..., tq:tq + half], p_[..., half:]], axis=-1)
                     for f, p_ in zip(carry, part))
        o = acc / l
        o = jnp.transpose(o[:, :tq] - lam * o[:, tq:])
        o_ref[pl.ds(pl.multiple_of(qi * tq, tq), tq), :] = _rms(o, sg_ref[...]) * (1.0 - lam_init)
        return 0

    lax.fori_loop(0, nq, q_tile, 0)


def attention_prompt(qt, k, vt, lam_vecs, subln_g, slopes, lam_init):
    b, nq, hd, tq = qt.shape
    l = nq * tq
    dv = subln_g.shape[0]
    heads = hd // dv
    return pl.pallas_call(
        functools.partial(_attn_prompt_kernel, tq=tq, lam_init=lam_init),
        grid_spec=pltpu.PrefetchScalarGridSpec(
            num_scalar_prefetch=1,
            grid=(b, heads),
            in_specs=[
                pl.BlockSpec((None, nq, dv, tq), lambda i, h, s: (i, 0, h, 0)),
                pl.BlockSpec((None, l, dv), lambda i, h, s: (i, 0, h)),
                pl.BlockSpec((None, nq, dv, tq), lambda i, h, s: (i, 0, h, 0)),
                pl.BlockSpec(lam_vecs.shape, lambda i, h, s: (0, 0)),
                pl.BlockSpec((1, dv), lambda i, h, s: (0, 0)),
            ],
            out_specs=pl.BlockSpec((None, l, dv), lambda i, h, s: (i, 0, h)),
        ),
        out_shape=jax.ShapeDtypeStruct((b, l, hd), F32),
        compiler_params=_params("parallel", "parallel"),
        name="attention_prompt",
    )(slopes, qt, k, vt, lam_vecs, subln_g.reshape(1, dv))


def _attn_decode_kernel(pt_ref, slope_ref, q_ref, kn_ref, vn_ref, lam_ref, sg_ref, table_ref, *rest,
                        pp, group, heads, page, n_new, lam_init):
    k_refs = rest[:pp]
    v_refs = rest[pp:2 * pp]
    o_ref = rest[2 * pp]
    m_ref, l_ref, acc_ref = rest[2 * pp + 1:]
    j = pl.program_id(1)
    nj = pl.num_programs(1)
    dv = sg_ref.shape[1]
    nr = 2 * SUBLANE
    pairs = page * heads
    slope_col = jnp.concatenate([jnp.full((nr, 1), slope_ref[h], F32) for h in range(heads)], axis=0)

    @pl.when(j == 0)
    def _():
        m_ref[...] = jnp.full(m_ref.shape, NEG, F32)
        l_ref[...] = jnp.zeros(l_ref.shape, F32)
        acc_ref[...] = jnp.zeros(acc_ref.shape, F32)

    qv = q_ref[...] * (ATT_HEAD_DIM ** -0.5)
    q2 = [_two_branch_q(qv[:, h * dv:(h + 1) * dv], SUBLANE) for h in range(heads)]
    q_all = jnp.concatenate(q2, axis=0)

    def update(s, shift, pv_of):
        m = m_ref[...]
        m_new = jnp.maximum(m, jnp.max(s, axis=-1, keepdims=True) + shift)
        alpha = jnp.exp(m - m_new)
        pr = jnp.exp(s - (m_new - shift))
        l_ref[...] = alpha * l_ref[...] + jnp.sum(pr, axis=-1, keepdims=True)
        acc_ref[...] = alpha * acc_ref[...] + pv_of(pr.astype(BF16))
        m_ref[...] = m_new

    for g0 in range(0, pp, group):
        ks = [k_refs[p][...].astype(BF16) for p in range(g0, g0 + group)]
        vs = [v_refs[p][...].astype(BF16) for p in range(g0, g0 + group)]
        s = jnp.concatenate([_dot_nt(q_all, kb) for kb in ks], axis=1) + table_ref[...]
        start = ((j * pp + g0) * page).astype(F32)
        update(s, slope_col * start,
               lambda pb: sum(_dot(pb[:, i * pairs:(i + 1) * pairs], vs[i]) for i in range(group)))

    @pl.when(j == nj - 1)
    def _():
        row = lax.broadcasted_iota(I32, (heads * nr, page), 0)
        col = lax.broadcasted_iota(I32, (heads * nr, page), 1)
        zeros = jnp.zeros((page - SUBLANE, kn_ref.shape[1]), F32)
        kb = jnp.concatenate([kn_ref[...], zeros], axis=0).astype(BF16)
        vb = jnp.concatenate([vn_ref[...], zeros], axis=0).astype(BF16)
        dist = (row % SUBLANE) - col
        ok = (dist >= 0) & (col < n_new)
        s = jnp.concatenate([_dot_nt(q2[h], kb[:, h * dv:(h + 1) * dv]) for h in range(heads)], axis=0)
        update(s + jnp.where(ok, -slope_col * dist.astype(F32), NEG), jnp.zeros_like(slope_col),
               lambda pb: jnp.concatenate([_dot(pb[h * nr:(h + 1) * nr], vb[:, h * dv:(h + 1) * dv])
                                           for h in range(heads)], axis=0))
        o = acc_ref[...] / l_ref[...]
        lam = _lambda(lam_ref, lam_init)
        for h in range(heads):
            oh = o[h * nr:h * nr + SUBLANE] - lam * o[h * nr + SUBLANE:(h + 1) * nr]
            o_ref[:, h * dv:(h + 1) * dv] = _rms(oh, sg_ref[...]) * (1.0 - lam_init)


def attention_decode(q, k_new, v_new, cache_k, cache_v, page_table, lam_vecs, subln_g, slopes, lam_init):
    b, n_new, hd = q.shape
    dv = subln_g.shape[0]
    heads = hd // dv
    n_pages = page_table.shape[1]
    page = cache_k.shape[1]
    pp = math.gcd(PAGES_PER_STEP, n_pages)
    assert n_new <= SUBLANE
    padr = ((0, 0), (0, SUBLANE - n_new), (0, 0))
    qp, kp, vp = (jnp.pad(a, padr) for a in (q, k_new, v_new))

    n_pool = cache_k.shape[0]
    ck, cv = (c.reshape(n_pool, page * heads, dv) for c in (cache_k, cache_v))

    group = math.gcd(DECODE_GROUP, pp)
    nrow = heads * 2 * SUBLANE
    r = jnp.arange(nrow, dtype=I32)[:, None]
    c = jnp.arange(group * page * heads, dtype=I32)[None, :]
    qpos = (n_pages * page + r % SUBLANE).astype(F32)
    table = jnp.where(r // (2 * SUBLANE) == c % heads,
                      -slopes[r // (2 * SUBLANE)] * (qpos - (c // heads).astype(F32)), NEG)

    def page_spec(p_i):
        return pl.BlockSpec((None, page * heads, dv),
                            lambda i, j, pt, s: (pt[i * n_pages + j * pp + p_i], 0, 0))

    row_spec = pl.BlockSpec((None, SUBLANE, hd), lambda i, j, pt, s: (i, 0, 0))
    out = pl.pallas_call(
        functools.partial(_attn_decode_kernel, pp=pp, group=group, heads=heads, page=page, n_new=n_new,
                          lam_init=lam_init),
        grid_spec=pltpu.PrefetchScalarGridSpec(
            num_scalar_prefetch=2,
            grid=(b, n_pages // pp),
            in_specs=[row_spec, row_spec, row_spec,
                      pl.BlockSpec(lam_vecs.shape, lambda i, j, pt, s: (0, 0)),
                      pl.BlockSpec((1, dv), lambda i, j, pt, s: (0, 0)),
                      pl.BlockSpec(table.shape, lambda i, j, pt, s: (0, 0), pipeline_mode=pl.Buffered(1))]
                     + [page_spec(p_i) for p_i in range(pp)] * 2,
            out_specs=row_spec,
            scratch_shapes=[pltpu.VMEM((heads * 2 * SUBLANE, 1), F32),
                            pltpu.VMEM((heads * 2 * SUBLANE, 1), F32),
                            pltpu.VMEM((heads * 2 * SUBLANE, dv), F32)],
        ),
        out_shape=jax.ShapeDtypeStruct((b, SUBLANE, hd), F32),
        compiler_params=_params("parallel", "arbitrary"),
        name="attention_decode",
    )(page_table.reshape(-1), slopes, qp, kp, vp, lam_vecs, subln_g.reshape(1, dv), table,
      *([ck] * pp), *([cv] * pp))
    return out[:, :n_new]


def _router_kernel(x_ref, g_ref, wr_ref, br_ref, h_ref, idx_ref, gate_ref, cend_ref, run_ref):
    i = pl.program_id(0)
    tm = x_ref.shape[0]

    @pl.when(i == 0)
    def _():
        run_ref[...] = jnp.zeros(run_ref.shape, F32)

    h = _rms(x_ref[...], g_ref[...])
    h_ref[...] = h.astype(BF16)
    logits = lax.dot_general(wr_ref[...], h, (((1,), (1,)), ((), ())), precision=lax.Precision.HIGHEST,
                             preferred_element_type=F32) + br_ref[...]
    ne = logits.shape[0]
    eid = lax.broadcasted_iota(I32, (ne, tm), 0)
    eidf = eid.astype(F32)
    v0 = jnp.max(logits, axis=0, keepdims=True)
    i0 = jnp.min(jnp.where(logits == v0, eidf, float(ne)), axis=0, keepdims=True).astype(I32)
    rest = jnp.where(eid == i0, -jnp.inf, logits)
    v1 = jnp.max(rest, axis=0, keepdims=True)
    i1 = jnp.min(jnp.where(rest == v1, eidf, float(ne)), axis=0, keepdims=True).astype(I32)
    e = jnp.exp(v1 - v0)
    g0 = 1.0 / (1.0 + e)
    g1 = e / (1.0 + e)
    sel0 = eid == i0
    sel1 = eid == i1
    assign = (sel0 | sel1).astype(BF16)
    before = (lax.broadcasted_iota(I32, (tm, tm), 0) < lax.broadcasted_iota(I32, (tm, tm), 1)).astype(BF16)
    rank = _dot(assign, before) + run_ref[...]
    r0 = jnp.sum(jnp.where(sel0, rank, 0.0), axis=0, keepdims=True).astype(I32)
    r1 = jnp.sum(jnp.where(sel1, rank, 0.0), axis=0, keepdims=True).astype(I32)
    idx_ref[...] = jnp.where(eid == 0, i0, jnp.where(eid == 1, i1, jnp.where(eid == 2, r0,
                             jnp.where(eid == 3, r1, 0))))
    gate_ref[...] = jnp.where(eid == 0, g0, jnp.where(eid == 1, g1, 0.0))
    run = run_ref[...] + jnp.sum(assign.astype(F32), axis=1, keepdims=True)
    run_ref[...] = run
    cend_ref[...] = jnp.broadcast_to(run, cend_ref.shape)


def moe_route(x, g, w_router, b_router, tc=MOE_CHUNK):
    t, d = x.shape
    ne = w_router.shape[1]
    assert t % tc == 0 and ne == SUBLANE
    nchunk = t // tc
    return pl.pallas_call(
        _router_kernel,
        grid=(nchunk,),
        in_specs=[pl.BlockSpec((tc, d), lambda i: (i, 0)), _resident((1, d)),
                  _resident((ne, d)), _resident((ne, 1))],
        out_specs=[pl.BlockSpec((tc, d), lambda i: (i, 0)),
                   pl.BlockSpec((ne, tc), lambda i: (0, i)),
                   pl.BlockSpec((ne, tc), lambda i: (0, i)),
                   pl.BlockSpec((None, ne, LANE), lambda i: (i, 0, 0))],
        out_shape=[jax.ShapeDtypeStruct((t, d), BF16), jax.ShapeDtypeStruct((ne, t), I32),
                   jax.ShapeDtypeStruct((ne, t), F32), jax.ShapeDtypeStruct((nchunk, ne, LANE), F32)],
        scratch_shapes=[pltpu.VMEM((ne, 1), F32)],
        compiler_params=_params("arbitrary"),
        name="moe_route",
    )(x, g.reshape(1, d), w_router.T, b_router.reshape(ne, 1))


def _slot_rows(idx_ref, off_ref, slot0, ne):
    idx = idx_ref[...]
    e0, e1, r0, r1 = idx[0:1], idx[1:2], idx[2:3], idx[3:4]
    o0 = jnp.zeros_like(e0)
    o1 = jnp.zeros_like(e1)
    for k in range(ne):
        o0 = jnp.where(e0 == k, off_ref[k], o0)
        o1 = jnp.where(e1 == k, off_ref[k], o1)
    return o0 + r0 - slot0, o1 + r1 - slot0


def _moe_ffn_kernel(wt_ref, wc_ref, wf_ref, te_ref, off_ref, idx_ref, h_ref, wg_ref, wu_ref, wd_ref,
                    o_ref, acc_ref, *, ts, ne):
    w = pl.program_id(0)
    flags = wf_ref[w]

    @pl.when((flags & 2) != 0)
    def _():
        acc_ref[...] = jnp.zeros(acc_ref.shape, F32)

    @pl.when((flags & 1) != 0)
    def _():
        sa, sb = _slot_rows(idx_ref, off_ref, wt_ref[w] * ts, ne)
        rows = lax.broadcasted_iota(I32, (ts, idx_ref.shape[1]), 0)
        pick = ((rows == sa) | (rows == sb)).astype(BF16)
        acc_ref[...] += _dot(pick, h_ref[...])

    @pl.when((flags & 4) != 0)
    def _():
        xb = acc_ref[...].astype(BF16)
        a = _dot(xb, wg_ref[...])
        u = _dot(xb, wu_ref[...])
        o_ref[...] = _dot((_silu(a) * u).astype(BF16), wd_ref[...]).astype(BF16)


def _moe_combine_kernel(wc_ref, wt_ref, wf_ref, off_ref, idx_ref, gate_ref, x_ref, es_ref, *rest,
                        ts, ne, final_norm):
    fg_ref = rest[0] if final_norm else None
    o_ref, acc_ref = rest[-2:]
    w = pl.program_id(0)
    flags = wf_ref[w]

    @pl.when((flags & 2) != 0)
    def _():
        acc_ref[...] = x_ref[...]

    @pl.when((flags & 1) != 0)
    def _():
        sa, sb = _slot_rows(idx_ref, off_ref, wt_ref[w] * ts, ne)
        gt = gate_ref[...]
        rows = lax.broadcasted_iota(I32, (ts, idx_ref.shape[1]), 0)
        wgt = (jnp.where(rows == sa, gt[0:1], 0.0) + jnp.where(rows == sb, gt[1:2], 0.0)).astype(BF16)
        acc_ref[...] += _dot_tn(wgt, es_ref[...])

    @pl.when((flags & 4) != 0)
    def _():
        o_ref[...] = _rms(acc_ref[...], fg_ref[...]) if final_norm else acc_ref[...]


def _work_lists(cend, t, tc, ts, ne):
    nchunk = t // tc
    nt_max = (2 * t) // ts + ne
    w_max = nt_max + ne * nchunk
    cend = cend.astype(I32)
    cstart = jnp.concatenate([jnp.zeros((1, ne), I32), cend[:-1]], axis=0)
    cnt = cend[-1]
    tiles = (cnt + ts - 1) // ts
    tile_end = jnp.cumsum(tiles)
    off = (tile_end - tiles) * ts
    n_tiles = tile_end[-1]
    tile_ids = jnp.arange(nt_max, dtype=I32)

    def count_le(sorted_vals, queries):
        return jnp.sum(sorted_vals[None, :] <= queries[:, None], axis=1).astype(I32)

    te = jnp.minimum(count_le(tile_end, tile_ids), ne - 1)

    def flatten(counts, w_total):
        ends = jnp.cumsum(counts)
        total = ends[-1]
        wi = jnp.clip(jnp.arange(w_total, dtype=I32), 0, jnp.maximum(total - 1, 0))
        owner = jnp.minimum(count_le(ends, wi), counts.shape[0] - 1)
        local = wi - (ends[owner] - counts[owner])
        valid = jnp.arange(w_total, dtype=I32) < total
        return owner, local, valid

    k0 = tile_ids * ts - off[te]
    k1 = jnp.minimum(k0 + ts, cnt[te])
    ce_t = cend[:, te]
    cs_t = cstart[:, te]
    c_lo = jnp.sum(ce_t <= k0[None, :], axis=0).astype(I32)
    c_hi = (nchunk - 1 - jnp.sum(cs_t >= k1[None, :], axis=0)).astype(I32)
    active = tile_ids < n_tiles
    n_items = jnp.where(active, c_hi - c_lo + 1, 0)
    owner, local, valid = flatten(n_items, w_max)
    f_tile, f_chunk = owner, jnp.clip(c_lo[owner] + local, 0, nchunk - 1)
    first = local == 0
    last = local == n_items[owner] - 1
    f_flags = (valid * (1 + 2 * first + 4 * last)).astype(I32)

    has = cend > cstart
    s_lo = (off[None, :] + cstart) // ts
    s_hi = (off[None, :] + cend - 1) // ts
    n_ce = jnp.where(has, s_hi - s_lo + 1, 0).reshape(-1)
    owner, local, valid = flatten(n_ce, w_max)
    c_chunk = owner // ne
    c_tile = jnp.clip(s_lo.reshape(-1)[owner] + local, 0, nt_max - 1)
    per_chunk = jnp.sum(n_ce.reshape(nchunk, ne), axis=1)
    chunk_end = jnp.cumsum(per_chunk)
    wi = jnp.clip(jnp.arange(w_max, dtype=I32), 0, jnp.maximum(chunk_end[-1] - 1, 0))
    first = wi == (chunk_end - per_chunk)[c_chunk]
    last = wi == chunk_end[c_chunk] - 1
    c_flags = (valid * (1 + 2 * first + 4 * last)).astype(I32)
    return (f_tile.astype(I32), f_chunk.astype(I32), f_flags, te, off.astype(I32),
            c_chunk.astype(I32), c_tile.astype(I32), c_flags, nt_max, w_max)


def moe(x, g, w_router, b_router, wg, wu, wd, layer, final_g=None, tc=MOE_CHUNK, ts=MOE_SLOT_TILE):
    t, d = x.shape
    _, ne, _, f = wg.shape
    tc = min(tc, t)
    h, idx, gate, cend = moe_route(x, g, w_router, b_router, tc)
    (f_tile, f_chunk, f_flags, te, off, c_chunk, c_tile, c_flags, nt_max, w_max) = _work_lists(
        cend[:, :, 0], t, tc, ts, ne)

    sorted_out = pl.pallas_call(
        functools.partial(_moe_ffn_kernel, ts=ts, ne=ne),
        grid_spec=pltpu.PrefetchScalarGridSpec(
            num_scalar_prefetch=5,
            grid=(w_max,),
            in_specs=[
                pl.BlockSpec((ne, tc), lambda w, wt, wc, wf, te_, of: (0, wc[w])),
                pl.BlockSpec((tc, d), lambda w, wt, wc, wf, te_, of: (wc[w], 0)),
                pl.BlockSpec((None, None, d, f), lambda w, wt, wc, wf, te_, of: (layer, te_[wt[w]], 0, 0)),
                pl.BlockSpec((None, None, d, f), lambda w, wt, wc, wf, te_, of: (layer, te_[wt[w]], 0, 0)),
                pl.BlockSpec((None, None, f, d), lambda w, wt, wc, wf, te_, of: (layer, te_[wt[w]], 0, 0)),
            ],
            out_specs=pl.BlockSpec((ts, d), lambda w, wt, wc, wf, te_, of: (wt[w], 0)),
            scratch_shapes=[pltpu.VMEM((ts, d), F32)],
        ),
        out_shape=jax.ShapeDtypeStruct((nt_max * ts, d), BF16),
        compiler_params=_params("arbitrary"),
        name="moe_ffn",
    )(f_tile, f_chunk, f_flags, te, off, idx, h, wg, wu, wd)

    return pl.pallas_call(
        functools.partial(_moe_combine_kernel, ts=ts, ne=ne, final_norm=final_g is not None),
        grid_spec=pltpu.PrefetchScalarGridSpec(
            num_scalar_prefetch=4,
            grid=(w_max,),
            in_specs=[
                pl.BlockSpec((ne, tc), lambda w, wc, wt, wf, of: (0, wc[w])),
                pl.BlockSpec((ne, tc), lambda w, wc, wt, wf, of: (0, wc[w])),
                pl.BlockSpec((tc, d), lambda w, wc, wt, wf, of: (wc[w], 0)),
                pl.BlockSpec((ts, d), lambda w, wc, wt, wf, of: (wt[w], 0)),
            ] + ([pl.BlockSpec((1, d), lambda w, wc, wt, wf, of: (0, 0))] if final_g is not None else []),
            out_specs=pl.BlockSpec((tc, d), lambda w, wc, wt, wf, of: (wc[w], 0)),
            scratch_shapes=[pltpu.VMEM((tc, d), F32)],
        ),
        out_shape=jax.ShapeDtypeStruct((t, d), F32),
        compiler_params=_params("arbitrary"),
        name="moe_combine",
    )(c_chunk, c_tile, c_flags, off, idx, gate, x, sorted_out,
      *([final_g.reshape(1, d)] if final_g is not None else []))


def _lambda_init(layer):
    return 0.8 - 0.6 * math.exp(-0.3 * layer)


def kernel(x_prompt, x_sample, state_ssm, state_conv, cache_k, cache_v, page_table, norm_mix_g, norm_ffn_g, norm_kv_g, norm_final_g, ssm_w_in, ssm_conv_w, ssm_conv_b, ssm_dt_bias, ssm_a_log, ssm_d, ssm_norm_g, ssm_w_out, kv_w_k, kv_w_v, att_w_q, att_lam_q1, att_lam_k1, att_lam_q2, att_lam_k2, att_subln_g, att_w_o, ffn_w_gate, ffn_w_up, ffn_w_down, moe_w_router, moe_b_router, moe_w_gate, moe_w_up, moe_w_down):
    depth, d_model = norm_mix_g.shape
    n_a = ssm_w_in.shape[0]
    heads_ssm = ssm_a_log.shape[1]
    d_inner = heads_ssm * SSM_HEAD_DIM
    conv_dim = ssm_conv_w.shape[2]
    att_dv = att_subln_g.shape[1]
    att_heads = kv_w_v.shape[1] // att_dv
    slopes = 2.0 ** (-8.0 * jnp.arange(1, att_heads + 1, dtype=F32) / att_heads)

    def per_layer(w):
        stack = w.astype(BF16)
        return [(stack, i) for i in range(w.shape[0])]

    in_dim = ssm_w_in.shape[2]
    w_in = per_layer(jnp.pad(ssm_w_in, ((0, 0), (0, 0), (0, d_inner + conv_dim + LANE - in_dim))))
    w_out = per_layer(ssm_w_out)
    w_k, w_v = kv_w_k.astype(BF16), kv_w_v.astype(BF16)
    w_q, w_o = per_layer(att_w_q), per_layer(att_w_o)
    f_gate, f_up, f_down = per_layer(ffn_w_gate), per_layer(ffn_w_up), per_layer(ffn_w_down)
    m_gate, m_up, m_down = moe_w_gate.astype(BF16), moe_w_up.astype(BF16), moe_w_down.astype(BF16)
    lam_vecs = jnp.stack([att_lam_q1, att_lam_k1, att_lam_q2, att_lam_k2], axis=1)

    def trunk(x3, ssm0, conv0, past):
        b, l, _ = x3.shape
        x = x3.reshape(b * l, d_model)
        states = None
        k_new = v_new = k_att = v_att = None
        for layer in range(depth):
            if layer < n_a:
                ssd_w = (ssm_conv_w[layer], ssm_conv_b[layer], ssm_dt_bias[layer], ssm_a_log[layer],
                         ssm_d[layer], ssm_norm_g[layer])
                if l % SSD_CHUNK == 0:
                    xo, *states = ssd_mixer(x.reshape(b, l, d_model), layer, n_a, conv0, ssm0, states, *ssd_w,
                                            SSD_CHUNK, fuse=(norm_mix_g[layer], w_in[layer], w_out[layer]))
                    x = xo.reshape(b * l, d_model)
                else:
                    zx = linear(x, w_in[layer], g=norm_mix_g[layer]).reshape(b, l, -1)
                    y, *states = ssd_mixer(zx, layer, n_a, conv0, ssm0, states, *ssd_w, l)
                    x = linear(y.reshape(b * l, d_inner), w_out[layer], residual=x)
            else:
                j = layer - n_a
                lam_init = _lambda_init(layer)
                if past is None:
                    qt = linear(x, w_q[j], g=norm_mix_g[layer], scale=ATT_HEAD_DIM ** -0.5, outs=("bf16_t",),
                                tm=ATT_TILE)
                    o = attention_prompt(qt.reshape(b, l // ATT_TILE, -1, ATT_TILE), k_att, v_att,
                                         lam_vecs[j], att_subln_g[j], slopes, lam_init)
                else:
                    q = linear(x, w_q[j], g=norm_mix_g[layer])
                    o = attention_decode(q.reshape(b, l, -1), k_att, v_att, past[0], past[1], page_table,
                                         lam_vecs[j], att_subln_g[j], slopes, lam_init)
                x = linear(o.reshape(b * l, -1), w_o[j], residual=x)
            i = layer // 2
            if layer % 2 == 0:
                x = ffn(x, norm_ffn_g[layer], f_gate[i], f_up[i], f_down[i])
            else:
                x = moe(x, norm_ffn_g[layer], moe_w_router[i], moe_b_router[i], m_gate, m_up, m_down, i,
                        final_g=norm_final_g if layer == depth - 1 else None)
            if layer == n_a - 1:
                if past is None:
                    assert l % ATT_TILE == 0
                    k_new, k_att = linear(x, w_k, g=norm_kv_g, outs=("f32", "bf16"))
                    v_new, v_att = linear(x, w_v, g=norm_kv_g, outs=("f32", "bf16_t"), tm=ATT_TILE)
                    k_att = k_att.reshape(b, l, -1)
                    v_att = v_att.reshape(b, l // ATT_TILE, -1, ATT_TILE)
                else:
                    k_att = k_new = linear(x, w_k, g=norm_kv_g).reshape(b, l, -1)
                    v_att = v_new = linear(x, w_v, g=norm_kv_g).reshape(b, l, -1)
        y = (x if depth % 2 == 0 else rmsnorm(x, norm_final_g)).reshape(b, l, d_model)
        return (y, states[0].reshape(n_a, b, heads_ssm, SSM_HEAD_DIM, SSM_STATE), states[1],
                k_new.reshape(b, l, att_heads, att_dv), v_new.reshape(b, l, att_heads, att_dv))

    y_p, ssm_p, conv_p, k_p, v_p = trunk(x_prompt, None, None, None)
    bs = x_sample.shape[0]
    y_s, ssm_s, conv_s, k_s, v_s = trunk(x_sample, state_ssm.reshape(n_a, bs, d_inner, SSM_STATE), state_conv,
                                         (cache_k, cache_v))
    return (y_p, y_s, ssm_p, conv_p, k_p, v_p, ssm_s, conv_s, k_s, v_s)
```

```python
import functools
import math

import jax
import jax.numpy as jnp
from jax import lax
from jax.experimental import pallas as pl
from jax.experimental.pallas import tpu as pltpu

F32 = jnp.float32
BF16 = jnp.bfloat16
I32 = jnp.int32

RMS_EPS = 1e-5
SSM_HEAD_DIM = 64
SSM_GROUPS = 8
SSM_STATE = 128
CONV_WIDTH = 4
SSD_CHUNK = 128
ATT_HEAD_DIM = 64
N_EXPERTS = 8
LANE = 128
SUBLANE = 8
VMEM_LIMIT = 56 * 1024 * 1024
NEG = -1e30

TOKEN_TILE = 256
MOE_SLOT_TILE = 256
MOE_CHUNK = 1024
ATT_TILE = 512
PAGES_PER_STEP = 16
DECODE_GROUP = 16
SSD_SEQS_PER_STEP = 8


def _params(*sem):
    return pltpu.CompilerParams(dimension_semantics=sem, vmem_limit_bytes=VMEM_LIMIT)


def _resident(shape):
    nd = len(shape)
    return pl.BlockSpec(shape, lambda *_: (0,) * nd, pipeline_mode=pl.Buffered(1))


def _weight(w):
    if isinstance(w, tuple):
        stack, i = w
        shape = stack.shape[1:]
        zeros = (0,) * len(shape)
        return stack, shape, pl.BlockSpec((None,) + shape, lambda *_: (i,) + zeros,
                                          pipeline_mode=pl.Buffered(1))
    return w, w.shape, _resident(w.shape)


def _rms(x, g):
    ms = jnp.mean(x * x, axis=-1, keepdims=True)
    return x * lax.rsqrt(ms + RMS_EPS) * g


def _silu(x):
    return (0.5 * x) * (1.0 + jnp.tanh(0.5 * x))


def _dot(a, b):
    return jnp.dot(a, b, preferred_element_type=F32)


def _dot_nt(a, b):
    return lax.dot_general(a, b, (((1,), (1,)), ((), ())), preferred_element_type=F32)


def _dot_tn(a, b):
    return lax.dot_general(a, b, (((0,), (0,)), ((), ())), preferred_element_type=F32)


def _split2(x):
    hi = x.astype(BF16)
    lo = (x - hi.astype(F32)).astype(BF16)
    return hi, lo


def _split3(x):
    hi = x.astype(BF16)
    r = x - hi.astype(F32)
    mid = r.astype(BF16)
    lo = (r - mid.astype(F32)).astype(BF16)
    return hi, mid, lo


def _linear_kernel(*refs, norm, residual, scale, outs):
    it = iter(refs)
    x_ref = next(it)
    g_ref = next(it) if norm else None
    w_ref = next(it)
    r_ref = next(it) if residual else None
    x = x_ref[...].astype(F32)
    if norm:
        x = _rms(x, g_ref[...])
    y = _dot(x.astype(BF16), w_ref[...])
    if residual:
        y = y + r_ref[...]
    if scale is not None:
        y = y * scale
    for kind in outs:
        o_ref = next(it)
        if kind == "f32":
            o_ref[...] = y
        elif kind == "bf16":
            o_ref[...] = y.astype(BF16)
        else:
            o_ref[...] = jnp.transpose(y).astype(BF16)


def linear(x, w, g=None, residual=None, scale=None, outs=("f32",), tm=TOKEN_TILE):
    t, k = x.shape
    w, (_, n), w_spec = _weight(w)
    tm = min(tm, t)
    assert t % tm == 0
    args = [x]
    specs = [pl.BlockSpec((tm, k), lambda i: (i, 0))]
    if g is not None:
        args.append(g.reshape(1, k))
        specs.append(_resident((1, k)))
    args.append(w)
    specs.append(w_spec)
    if residual is not None:
        args.append(residual)
        specs.append(pl.BlockSpec((tm, n), lambda i: (i, 0)))
    out_specs, out_shapes = [], []
    for kind in outs:
        if kind == "bf16_t":
            out_specs.append(pl.BlockSpec((None, n, tm), lambda i: (i, 0, 0)))
            out_shapes.append(jax.ShapeDtypeStruct((t // tm, n, tm), BF16))
        else:
            out_specs.append(pl.BlockSpec((tm, n), lambda i: (i, 0)))
            out_shapes.append(jax.ShapeDtypeStruct((t, n), F32 if kind == "f32" else BF16))
    res = pl.pallas_call(
        functools.partial(_linear_kernel, norm=g is not None, residual=residual is not None,
                          scale=scale, outs=tuple(outs)),
        grid=(t // tm,),
        in_specs=specs,
        out_specs=out_specs,
        out_shape=out_shapes,
        compiler_params=_params("parallel"),
        name="linear",
    )(*args)
    return res[0] if len(outs) == 1 else res


def _ffn_kernel(x_ref, g_ref, wg_ref, wu_ref, wd_ref, o_ref):
    x = x_ref[...]
    h = _rms(x, g_ref[...]).astype(BF16)
    a = _dot(h, wg_ref[...])
    u = _dot(h, wu_ref[...])
    act = (_silu(a) * u).astype(BF16)
    o_ref[...] = x + _dot(act, wd_ref[...])


def ffn(x, g, wg, wu, wd, tm=2 * TOKEN_TILE):
    t, d = x.shape
    (wg, _, wg_spec), (wu, _, wu_spec), (wd, _, wd_spec) = _weight(wg), _weight(wu), _weight(wd)
    tm = min(tm, t)
    assert t % tm == 0
    return pl.pallas_call(
        _ffn_kernel,
        grid=(t // tm,),
        in_specs=[pl.BlockSpec((tm, d), lambda i: (i, 0)), _resident((1, d)), wg_spec, wu_spec, wd_spec],
        out_specs=pl.BlockSpec((tm, d), lambda i: (i, 0)),
        out_shape=jax.ShapeDtypeStruct((t, d), F32),
        compiler_params=_params("parallel"),
        name="ffn",
    )(x, g.reshape(1, d), wg, wu, wd)


def _rmsnorm_kernel(x_ref, g_ref, o_ref):
    o_ref[...] = _rms(x_ref[...], g_ref[...])


def rmsnorm(x, g, tm=TOKEN_TILE):
    t, d = x.shape
    tm = min(tm, t)
    assert t % tm == 0
    return pl.pallas_call(
        _rmsnorm_kernel,
        grid=(t // tm,),
        in_specs=[pl.BlockSpec((tm, d), lambda i: (i, 0)), _resident((1, d))],
        out_specs=pl.BlockSpec((tm, d), lambda i: (i, 0)),
        out_shape=jax.ShapeDtypeStruct((t, d), F32),
        compiler_params=_params("parallel"),
        name="rmsnorm",
    )(x, g.reshape(1, d))


def _ssd_kernel(*refs, q, rows, valid, d_inner, heads, fused, has_init, has_prev, single_chunk):
    it = iter(refs)
    src_ref = next(it)
    if fused:
        gmix_ref, win_ref, wout_ref = next(it), next(it), next(it)
    if has_init:
        conv0_ref, ssm0_ref = next(it), next(it)
    cw_ref, cb_ref, dtb_ref, alog_ref, dsk_ref, ng_ref, exp_ref = (next(it) for _ in range(7))
    if has_prev:
        next(it), next(it)
    out_ref, sfin_ref, cnew_ref, xp_ref, stage_ref = (next(it) for _ in range(5))
    ybuf_ref = next(it) if fused else None

    c = pl.program_id(1)
    nc = pl.num_programs(1)
    kp = SSD_CHUNK
    gn = SSM_GROUPS * SSM_STATE
    hpg = heads // SSM_GROUPS
    gw = hpg * SSM_HEAD_DIM
    halo = CONV_WIDTH - 1

    def when(cond):
        return (lambda f: f()) if single_chunk else pl.when(cond)

    @when(c == 0)
    def _():
        xp_ref[0:SUBLANE, :] = jnp.zeros((SUBLANE, xp_ref.shape[1]), F32)
        if has_init:
            sfin_ref[...] = ssm0_ref[...]
            xp_ref[SUBLANE - halo:SUBLANE, :] = conv0_ref[...]
        else:
            sfin_ref[...] = jnp.zeros(sfin_ref.shape, F32)

    if rows < q:
        stage_ref[...] = jnp.zeros(stage_ref.shape, F32)
        stage_ref[0:rows, :] = src_ref[...]
        src = stage_ref[...]
    else:
        src = src_ref[...]
    if fused:
        zx = _dot(_rms(src, gmix_ref[...]).astype(BF16), win_ref[...])
    else:
        zx = src
    z = zx[:, :d_inner]
    xbc = zx[:, d_inner:2 * d_inner + 2 * gn]
    dtr = zx[:, 2 * d_inner + 2 * gn:]

    xp_ref[SUBLANE:SUBLANE + q, :] = xbc
    xall = xp_ref[...]
    conv = cb_ref[...] + xall[SUBLANE:] * cw_ref[halo:halo + 1, :]
    for shift in range(1, CONV_WIDTH):
        k = halo - shift
        conv = conv + pltpu.roll(xall, shift, 0)[SUBLANE:] * cw_ref[k:k + 1, :]
    tail = xp_ref[SUBLANE + valid - halo:SUBLANE + valid, :]
    xp_ref[SUBLANE - halo:SUBLANE, :] = tail

    @when(c == nc - 1)
    def _():
        cnew_ref[...] = tail

    xa = _silu(conv)
    xs = xa[:, :d_inner]
    bm = xa[:, d_inner:d_inner + gn]
    cm = xa[:, d_inner + gn:]

    def pad(v):
        if q == kp:
            return v
        return jnp.concatenate([v, jnp.zeros((kp - q, v.shape[1]), v.dtype)], axis=0)

    row = lax.broadcasted_iota(I32, (q, kp), 0)
    col = lax.broadcasted_iota(I32, (q, kp), 1)
    causal = col <= row

    dtv = dtr + dtb_ref[...]
    dt = jnp.maximum(dtv, 0.0) + jnp.log1p(jnp.exp(-jnp.abs(dtv)))
    if valid < q:
        dt = jnp.where(lax.broadcasted_iota(I32, (q, LANE), 0) < valid, dt, 0.0)
    adt = dt * (-jnp.exp(alog_ref[...]))
    tril = causal.astype(BF16)
    cs = sum(_dot(tril, part) for part in _split3(pad(adt)))
    cs_t = jnp.transpose(pad(cs))
    cs_last = cs[q - 1:q, :]
    e_last = jnp.exp(cs_last)

    stacked = jnp.concatenate([dt, jnp.exp(cs), jnp.exp(cs_last - cs)], axis=0)
    hi, lo = _split2(stacked)
    wide = _dot(hi, exp_ref[...]) + _dot(lo, exp_ref[...])
    dt_w, ecs_w, dout_w = wide[0:q], wide[q:2 * q], wide[2 * q:3 * q]
    xdt = xs * dt_w
    xdtd = pad(xdt * dout_w).astype(BF16)
    xdt_b = pad(xdt).astype(BF16)
    bm_p = pad(bm).astype(BF16)
    cm_b = cm.astype(BF16)
    lane_g = lax.broadcasted_iota(I32, (kp, gw), 1) // SSM_HEAD_DIM
    head_mask = [(lane_g == r).astype(BF16) for r in range(hpg)]

    for g in range(SSM_GROUPS):
        b_g = bm_p[:, g * SSM_STATE:(g + 1) * SSM_STATE]
        c_g = cm_b[:, g * SSM_STATE:(g + 1) * SSM_STATE]
        cb = _dot_nt(c_g, b_g)
        xg = xdt_b[:, g * gw:(g + 1) * gw]
        m_parts, x_parts, scale_parts = [], [], []
        for r in range(hpg):
            h = g * hpg + r
            seg = cs[:, h:h + 1] - cs_t[h:h + 1, :]
            decay = jnp.exp(jnp.where(causal, seg, NEG))
            m_parts.append((cb * decay).astype(BF16))
            x_parts.append(xg * head_mask[r])
            scale_parts.append(jnp.broadcast_to(e_last[:, h:h + 1], (SSM_HEAD_DIM, SSM_STATE)))
        y_diag = _dot(jnp.concatenate(m_parts, axis=1), jnp.concatenate(x_parts, axis=0))
        s_g = sfin_ref[g * gw:(g + 1) * gw, :]
        y_off = _dot_nt(c_g, s_g.astype(BF16)) * ecs_w[:, g * gw:(g + 1) * gw]
        new = _dot_tn(xdtd[:, g * gw:(g + 1) * gw], b_g)
        sfin_ref[g * gw:(g + 1) * gw, :] = s_g * jnp.concatenate(scale_parts, axis=0) + new
        y = y_diag + y_off + dsk_ref[:, g * gw:(g + 1) * gw] * xs[:, g * gw:(g + 1) * gw]
        y = y * _silu(z[:, g * gw:(g + 1) * gw])
        y = y * lax.rsqrt(jnp.mean(y * y, axis=-1, keepdims=True) + RMS_EPS)
        y = y * ng_ref[:, g * gw:(g + 1) * gw]
        if fused:
            ybuf_ref[:, g * gw:(g + 1) * gw] = y.astype(BF16)
        else:
            out_ref[:, g * gw:(g + 1) * gw] = y

    if fused:
        out_ref[...] = src + _dot(ybuf_ref[...], wout_ref[...])


def _ssd_step_kernel(*refs, nb, per_seq, **kw):
    for sb in range(nb):
        _ssd_kernel(*[r.at[sb] if own else r for r, own in zip(refs, per_seq)], **kw)


def ssd_mixer(src, layer, n_layers, conv0_all, ssm0_all, prev, conv_w, conv_b, dt_bias, a_log, d_skip,
              norm_g, valid, fuse=None):
    b, l, width = src.shape
    heads = a_log.shape[0]
    d_inner = heads * SSM_HEAD_DIM
    conv_dim = conv_w.shape[1]
    if l % SSD_CHUNK == 0:
        q = rows = SSD_CHUNK
        nc = l // SSD_CHUNK
        assert valid == SSD_CHUNK
    else:
        assert l < SSD_CHUNK
        rows, nc = l, 1
        q = -(-l // SUBLANE) * SUBLANE
    padh = LANE - heads
    expand = jnp.repeat(jnp.eye(LANE, heads, dtype=BF16), SSM_HEAD_DIM, axis=1)
    halo = CONV_WIDTH - 1

    nb = math.gcd(SSD_SEQS_PER_STEP, b) if nc == 1 else 1
    args = [src]
    specs = [pl.BlockSpec((nb, rows, width), lambda i, j: (i, j, 0))]
    per_seq = [True]
    if fuse is not None:
        g_mix, w_in, w_out = fuse
        (w_in, in_shape, in_spec), (w_out, out_shape, out_spec) = _weight(w_in), _weight(w_out)
        assert in_shape == (width, d_inner + conv_dim + LANE)
        args += [g_mix.reshape(1, width), w_in, w_out]
        specs += [_resident((1, width)), in_spec, out_spec]
        per_seq += [False] * 3
        out_width = out_shape[1]
    else:
        assert width == d_inner + conv_dim + LANE
        out_width = d_inner
    if ssm0_all is not None:
        args += [conv0_all, ssm0_all]
        specs += [pl.BlockSpec((None, nb, halo, conv_dim), lambda i, j: (layer, i, 0, 0)),
                  pl.BlockSpec((None, nb, d_inner, SSM_STATE), lambda i, j: (layer, i, 0, 0))]
        per_seq += [True] * 2
    args += [conv_w, conv_b.reshape(1, conv_dim),
             jnp.pad(dt_bias, (0, padh)).reshape(1, LANE), jnp.pad(a_log, (0, padh)).reshape(1, LANE),
             jnp.repeat(d_skip, SSM_HEAD_DIM).reshape(1, d_inner), norm_g.reshape(1, d_inner), expand]
    specs += [_resident((CONV_WIDTH, conv_dim)), _resident((1, conv_dim)), _resident((1, LANE)),
              _resident((1, LANE)), _resident((1, d_inner)), _resident((1, d_inner)),
              _resident((LANE, d_inner))]
    per_seq += [False] * 7
    aliases = {}
    if prev is not None:
        aliases = {len(args): 1, len(args) + 1: 2}
        args += list(prev)
        specs += [pl.BlockSpec(memory_space=pl.ANY), pl.BlockSpec(memory_space=pl.ANY)]
        per_seq += [False] * 2
    per_seq += [True] * 3
    scratch = [pltpu.VMEM((nb, SUBLANE + q, conv_dim), F32),
               pltpu.VMEM((nb, q, width) if rows < q else (nb, SUBLANE, LANE), F32)]
    per_seq += [True] * 2
    if fuse is not None:
        scratch.append(pltpu.VMEM((q, d_inner), BF16))
        per_seq.append(False)
    kern = functools.partial(_ssd_step_kernel, nb=nb, per_seq=tuple(per_seq), q=q, rows=rows, valid=valid,
                             d_inner=d_inner, heads=heads, fused=fuse is not None,
                             has_init=ssm0_all is not None, has_prev=prev is not None, single_chunk=nc == 1)
    out, sfin, cnew = pl.pallas_call(
        kern,
        grid=(b // nb, nc),
        in_specs=specs,
        out_specs=[
            pl.BlockSpec((nb, q, out_width), lambda i, j: (i, j, 0)),
            pl.BlockSpec((None, nb, d_inner, SSM_STATE), lambda i, j: (layer, i, 0, 0)),
            pl.BlockSpec((None, nb, halo, conv_dim), lambda i, j: (layer, i, 0, 0)),
        ],
        out_shape=[
            jax.ShapeDtypeStruct((b, nc * q, out_width), F32),
            jax.ShapeDtypeStruct((n_layers, b, d_inner, SSM_STATE), F32),
            jax.ShapeDtypeStruct((n_layers, b, halo, conv_dim), F32),
        ],
        scratch_shapes=scratch,
        input_output_aliases=aliases,
        compiler_params=_params("parallel", "arbitrary"),
        name="ssd_mixer",
    )(*args)
    return out[:, :l], sfin, cnew


def _lambda(lam_ref, lam_init):
    lv = lam_ref[...]
    s1 = jnp.sum(lv[0:1] * lv[1:2], axis=-1, keepdims=True)
    s2 = jnp.sum(lv[2:3] * lv[3:4], axis=-1, keepdims=True)
    return jnp.exp(s1) - jnp.exp(s2) + lam_init


def _two_branch_q(qv, n):
    lane = lax.broadcasted_iota(I32, qv.shape, 1)
    zero = jnp.zeros_like(qv)
    return jnp.concatenate([jnp.where(lane < ATT_HEAD_DIM, qv, zero),
                            jnp.where(lane >= ATT_HEAD_DIM, qv, zero)], axis=0).astype(BF16)


def _attn_prompt_kernel(slope_ref, qt_ref, k_ref, vt_ref, lam_ref, sg_ref, o_ref, *, tq, lam_init):
    slope = slope_ref[pl.program_id(1)]
    nq, dv, _ = qt_ref.shape
    feat = lax.broadcasted_iota(I32, (dv, tq), 0)
    cc = lax.broadcasted_iota(I32, (tq, 2 * tq), 1)
    rel = jnp.where(cc >= tq, cc - tq, cc) - lax.broadcasted_iota(I32, (tq, 2 * tq), 0)
    bias = -slope * rel.astype(F32)
    bias_diag = jnp.where(rel >= 0, bias, NEG)
    lam = _lambda(lam_ref, lam_init)
    m0 = jnp.full((1, 2 * tq), NEG, F32)
    l0 = jnp.zeros((1, 2 * tq), F32)
    a0 = jnp.zeros((dv, 2 * tq), F32)

    def q_tile(qi, _):
        qt = qt_ref[qi]
        zero = jnp.zeros_like(qt)
        q2t = jnp.concatenate([jnp.where(feat < ATT_HEAD_DIM, qt, zero),
                               jnp.where(feat >= ATT_HEAD_DIM, qt, zero)], axis=1)

        def scores(j):
            start = pl.multiple_of(j * tq, tq)
            return _dot(k_ref[pl.ds(start, tq), :], q2t)

        def update(s, vt, carry, tile_bias):
            m, l, acc = carry
            m_new = jnp.maximum(m, jnp.max(s, axis=0, keepdims=True) + tile_bias)
            alpha = jnp.exp(m - m_new)
            p = jnp.exp(s - (m_new - tile_bias))
            l = alpha * l + jnp.sum(p, axis=0, keepdims=True)
            acc = alpha * acc + _dot(vt, p.astype(BF16))
            return m_new, l, acc

        def body(j, c):
            s_next = scores(j + 1)
            tile_bias = -slope * ((qi - j) * tq).astype(F32)
            return update(c[3] + bias, vt_ref[j], c[:3], tile_bias) + (s_next,)

        c = lax.fori_loop(0, qi, body, (m0, l0, a0, scores(0)))
        half = tq // 2
        raw, vt_d = c[3], vt_ref[qi]
        carry = update(raw[:half] + bias_diag[:half], vt_d[:, :half], c[:3], 0.0)

        def late(a):
            return jnp.concatenate([a[..., half:tq], a[..., tq + half:]], axis=-1)

        part = update(late(raw[half:]) + late(bias_diag[half:]), vt_d[:, half:],
                      tuple(late(a) for a in carry), 0.0)
        _, l, acc = (jnp.concatenate([f[..., :half], p_[..., :half], f[..., tq:tq + half], p_[..., half:]], axis=-1)
                     for f, p_ in zip(carry, part))
        o = acc / l
        o = jnp.transpose(o[:, :tq] - lam * o[:, tq:])
        o_ref[pl.ds(pl.multiple_of(qi * tq, tq), tq), :] = _rms(o, sg_ref[...]) * (1.0 - lam_init)
        return 0

    lax.fori_loop(0, nq, q_tile, 0)


def attention_prompt(qt, k, vt, lam_vecs, subln_g, slopes, lam_init):
    b, nq, hd, tq = qt.shape
    l = nq * tq
    dv = subln_g.shape[0]
    heads = hd // dv
    return pl.pallas_call(
        functools.partial(_attn_prompt_kernel, tq=tq, lam_init=lam_init),
        grid_spec=pltpu.PrefetchScalarGridSpec(
            num_scalar_prefetch=1,
            grid=(b, heads),
            in_specs=[
                pl.BlockSpec((None, nq, dv, tq), lambda i, h, s: (i, 0, h, 0)),
                pl.BlockSpec((None, l, dv), lambda i, h, s: (i, 0, h)),
                pl.BlockSpec((None, nq, dv, tq), lambda i, h, s: (i, 0, h, 0)),
                pl.BlockSpec(lam_vecs.shape, lambda i, h, s: (0, 0)),
                pl.BlockSpec((1, dv), lambda i, h, s: (0, 0)),
            ],
            out_specs=pl.BlockSpec((None, l, dv), lambda i, h, s: (i, 0, h)),
        ),
        out_shape=jax.ShapeDtypeStruct((b, l, hd), F32),
        compiler_params=_params("parallel", "parallel"),
        name="attention_prompt",
    )(slopes, qt, k, vt, lam_vecs, subln_g.reshape(1, dv))


def _attn_decode_kernel(pt_ref, slope_ref, q_ref, kn_ref, vn_ref, lam_ref, sg_ref, table_ref, *rest,
                        pp, group, heads, page, n_new, lam_init):
    k_refs = rest[:pp]
    v_refs = rest[pp:2 * pp]
    o_ref = rest[2 * pp]
    m_ref, l_ref, acc_ref = rest[2 * pp + 1:]
    j = pl.program_id(1)
    nj = pl.num_programs(1)
    dv = sg_ref.shape[1]
    nr = 2 * SUBLANE
    pairs = page * heads
    slope_col = jnp.concatenate([jnp.full((nr, 1), slope_ref[h], F32) for h in range(heads)], axis=0)

    @pl.when(j == 0)
    def _():
        m_ref[...] = jnp.full(m_ref.shape, NEG, F32)
        l_ref[...] = jnp.zeros(l_ref.shape, F32)
        acc_ref[...] = jnp.zeros(acc_ref.shape, F32)

    qv = q_ref[...] * (ATT_HEAD_DIM ** -0.5)
    q2 = [_two_branch_q(qv[:, h * dv:(h + 1) * dv], SUBLANE) for h in range(heads)]
    q_all = jnp.concatenate(q2, axis=0)

    def update(s, shift, pv_of):
        m = m_ref[...]
        m_new = jnp.maximum(m, jnp.max(s, axis=-1, keepdims=True) + shift)
        alpha = jnp.exp(m - m_new)
        pr = jnp.exp(s - (m_new - shift))
        l_ref[...] = alpha * l_ref[...] + jnp.sum(pr, axis=-1, keepdims=True)
        acc_ref[...] = alpha * acc_ref[...] + pv_of(pr.astype(BF16))
        m_ref[...] = m_new

    for g0 in range(0, pp, group):
        ks = [k_refs[p][...].astype(BF16) for p in range(g0, g0 + group)]
        vs = [v_refs[p][...].astype(BF16) for p in range(g0, g0 + group)]
        s = jnp.concatenate([_dot_nt(q_all, kb) for kb in ks], axis=1) + table_ref[...]
        start = ((j * pp + g0) * page).astype(F32)
        update(s, slope_col * start,
               lambda pb: sum(_dot(pb[:, i * pairs:(i + 1) * pairs], vs[i]) for i in range(group)))

    @pl.when(j == nj - 1)
    def _():
        row = lax.broadcasted_iota(I32, (heads * nr, page), 0)
        col = lax.broadcasted_iota(I32, (heads * nr, page), 1)
        zeros = jnp.zeros((page - SUBLANE, kn_ref.shape[1]), F32)
        kb = jnp.concatenate([kn_ref[...], zeros], axis=0).astype(BF16)
        vb = jnp.concatenate([vn_ref[...], zeros], axis=0).astype(BF16)
        dist = (row % SUBLANE) - col
        ok = (dist >= 0) & (col < n_new)
        s = jnp.concatenate([_dot_nt(q2[h], kb[:, h * dv:(h + 1) * dv]) for h in range(heads)], axis=0)
        update(s + jnp.where(ok, -slope_col * dist.astype(F32), NEG), jnp.zeros_like(slope_col),
               lambda pb: jnp.concatenate([_dot(pb[h * nr:(h + 1) * nr], vb[:, h * dv:(h + 1) * dv])
                                           for h in range(heads)], axis=0))
        o = acc_ref[...] / l_ref[...]
        lam = _lambda(lam_ref, lam_init)
        for h in range(heads):
            oh = o[h * nr:h * nr + SUBLANE] - lam * o[h * nr + SUBLANE:(h + 1) * nr]
            o_ref[:, h * dv:(h + 1) * dv] = _rms(oh, sg_ref[...]) * (1.0 - lam_init)


def attention_decode(q, k_new, v_new, cache_k, cache_v, page_table, lam_vecs, subln_g, slopes, lam_init):
    b, n_new, hd = q.shape
    dv = subln_g.shape[0]
    heads = hd // dv
    n_pages = page_table.shape[1]
    page = cache_k.shape[1]
    pp = math.gcd(PAGES_PER_STEP, n_pages)
    assert n_new <= SUBLANE
    padr = ((0, 0), (0, SUBLANE - n_new), (0, 0))
    qp, kp, vp = (jnp.pad(a, padr) for a in (q, k_new, v_new))

    n_pool = cache_k.shape[0]
    ck, cv = (c.reshape(n_pool, page * heads, dv) for c in (cache_k, cache_v))

    group = math.gcd(DECODE_GROUP, pp)
    nrow = heads * 2 * SUBLANE
    r = jnp.arange(nrow, dtype=I32)[:, None]
    c = jnp.arange(group * page * heads, dtype=I32)[None, :]
    qpos = (n_pages * page + r % SUBLANE).astype(F32)
    table = jnp.where(r // (2 * SUBLANE) == c % heads,
                      -slopes[r // (2 * SUBLANE)] * (qpos - (c // heads).astype(F32)), NEG)

    def page_spec(p_i):
        return pl.BlockSpec((None, page * heads, dv),
                            lambda i, j, pt, s: (pt[i * n_pages + j * pp + p_i], 0, 0))

    row_spec = pl.BlockSpec((None, SUBLANE, hd), lambda i, j, pt, s: (i, 0, 0))
    out = pl.pallas_call(
        functools.partial(_attn_decode_kernel, pp=pp, group=group, heads=heads, page=page, n_new=n_new,
                          lam_init=lam_init),
        grid_spec=pltpu.PrefetchScalarGridSpec(
            num_scalar_prefetch=2,
            grid=(b, n_pages // pp),
            in_specs=[row_spec, row_spec, row_spec,
                      pl.BlockSpec(lam_vecs.shape, lambda i, j, pt, s: (0, 0)),
                      pl.BlockSpec((1, dv), lambda i, j, pt, s: (0, 0)),
                      pl.BlockSpec(table.shape, lambda i, j, pt, s: (0, 0), pipeline_mode=pl.Buffered(1))]
                     + [page_spec(p_i) for p_i in range(pp)] * 2,
            out_specs=row_spec,
            scratch_shapes=[pltpu.VMEM((heads * 2 * SUBLANE, 1), F32),
                            pltpu.VMEM((heads * 2 * SUBLANE, 1), F32),
                            pltpu.VMEM((heads * 2 * SUBLANE, dv), F32)],
        ),
        out_shape=jax.ShapeDtypeStruct((b, SUBLANE, hd), F32),
        compiler_params=_params("parallel", "arbitrary"),
        name="attention_decode",
    )(page_table.reshape(-1), slopes, qp, kp, vp, lam_vecs, subln_g.reshape(1, dv), table,
      *([ck] * pp), *([cv] * pp))
    return out[:, :n_new]


def _router_kernel(x_ref, g_ref, wr_ref, br_ref, h_ref, idx_ref, gate_ref, cend_ref, run_ref):
    i = pl.program_id(0)
    tm = x_ref.shape[0]

    @pl.when(i == 0)
    def _():
        run_ref[...] = jnp.zeros(run_ref.shape, F32)

    h = _rms(x_ref[...], g_ref[...])
    h_ref[...] = h.astype(BF16)
    logits = lax.dot_general(wr_ref[...], h, (((1,), (1,)), ((), ())), precision=lax.Precision.HIGHEST,
                             preferred_element_type=F32) + br_ref[...]
    ne = logits.shape[0]
    eid = lax.broadcasted_iota(I32, (ne, tm), 0)
    eidf = eid.astype(F32)
    v0 = jnp.max(logits, axis=0, keepdims=True)
    i0 = jnp.min(jnp.where(logits == v0, eidf, float(ne)), axis=0, keepdims=True).astype(I32)
    rest = jnp.where(eid == i0, -jnp.inf, logits)
    v1 = jnp.max(rest, axis=0, keepdims=True)
    i1 = jnp.min(jnp.where(rest == v1, eidf, float(ne)), axis=0, keepdims=True).astype(I32)
    e = jnp.exp(v1 - v0)
    g0 = 1.0 / (1.0 + e)
    g1 = e / (1.0 + e)
    sel0 = eid == i0
    sel1 = eid == i1
    assign = (sel0 | sel1).astype(BF16)
    before = (lax.broadcasted_iota(I32, (tm, tm), 0) < lax.broadcasted_iota(I32, (tm, tm), 1)).astype(BF16)
    rank = _dot(assign, before) + run_ref[...]
    r0 = jnp.sum(jnp.where(sel0, rank, 0.0), axis=0, keepdims=True).astype(I32)
    r1 = jnp.sum(jnp.where(sel1, rank, 0.0), axis=0, keepdims=True).astype(I32)
    idx_ref[...] = jnp.where(eid == 0, i0, jnp.where(eid == 1, i1, jnp.where(eid == 2, r0,
                             jnp.where(eid == 3, r1, 0))))
    gate_ref[...] = jnp.where(eid == 0, g0, jnp.where(eid == 1, g1, 0.0))
    run = run_ref[...] + jnp.sum(assign.astype(F32), axis=1, keepdims=True)
    run_ref[...] = run
    cend_ref[...] = jnp.broadcast_to(run, cend_ref.shape)


def moe_route(x, g, w_router, b_router, tc=MOE_CHUNK):
    t, d = x.shape
    ne = w_router.shape[1]
    assert t % tc == 0 and ne == SUBLANE
    nchunk = t // tc
    return pl.pallas_call(
        _router_kernel,
        grid=(nchunk,),
        in_specs=[pl.BlockSpec((tc, d), lambda i: (i, 0)), _resident((1, d)),
                  _resident((ne, d)), _resident((ne, 1))],
        out_specs=[pl.BlockSpec((tc, d), lambda i: (i, 0)),
                   pl.BlockSpec((ne, tc), lambda i: (0, i)),
                   pl.BlockSpec((ne, tc), lambda i: (0, i)),
                   pl.BlockSpec((None, ne, LANE), lambda i: (i, 0, 0))],
        out_shape=[jax.ShapeDtypeStruct((t, d), BF16), jax.ShapeDtypeStruct((ne, t), I32),
                   jax.ShapeDtypeStruct((ne, t), F32), jax.ShapeDtypeStruct((nchunk, ne, LANE), F32)],
        scratch_shapes=[pltpu.VMEM((ne, 1), F32)],
        compiler_params=_params("arbitrary"),
        name="moe_route",
    )(x, g.reshape(1, d), w_router.T, b_router.reshape(ne, 1))


def _slot_rows(idx_ref, off_ref, slot0, ne):
    idx = idx_ref[...]
    e0, e1, r0, r1 = idx[0:1], idx[1:2], idx[2:3], idx[3:4]
    o0 = jnp.zeros_like(e0)
    o1 = jnp.zeros_like(e1)
    for k in range(ne):
        o0 = jnp.where(e0 == k, off_ref[k], o0)
        o1 = jnp.where(e1 == k, off_ref[k], o1)
    return o0 + r0 - slot0, o1 + r1 - slot0


def _moe_ffn_kernel(wt_ref, wc_ref, wf_ref, te_ref, off_ref, idx_ref, h_ref, wg_ref, wu_ref, wd_ref,
                    o_ref, acc_ref, *, ts, ne):
    w = pl.program_id(0)
    flags = wf_ref[w]

    @pl.when((flags & 2) != 0)
    def _():
        acc_ref[...] = jnp.zeros(acc_ref.shape, F32)

    @pl.when((flags & 1) != 0)
    def _():
        sa, sb = _slot_rows(idx_ref, off_ref, wt_ref[w] * ts, ne)
        rows = lax.broadcasted_iota(I32, (ts, idx_ref.shape[1]), 0)
        pick = ((rows == sa) | (rows == sb)).astype(BF16)
        acc_ref[...] += _dot(pick, h_ref[...])

    @pl.when((flags & 4) != 0)
    def _():
        xb = acc_ref[...].astype(BF16)
        a = _dot(xb, wg_ref[...])
        u = _dot(xb, wu_ref[...])
        o_ref[...] = _dot((_silu(a) * u).astype(BF16), wd_ref[...]).astype(BF16)


def _moe_combine_kernel(wc_ref, wt_ref, wf_ref, off_ref, idx_ref, gate_ref, x_ref, es_ref, *rest,
                        ts, ne, final_norm):
    fg_ref = rest[0] if final_norm else None
    o_ref, acc_ref = rest[-2:]
    w = pl.program_id(0)
    flags = wf_ref[w]

    @pl.when((flags & 2) != 0)
    def _():
        acc_ref[...] = x_ref[...]

    @pl.when((flags & 1) != 0)
    def _():
        sa, sb = _slot_rows(idx_ref, off_ref, wt_ref[w] * ts, ne)
        gt = gate_ref[...]
        rows = lax.broadcasted_iota(I32, (ts, idx_ref.shape[1]), 0)
        wgt = (jnp.where(rows == sa, gt[0:1], 0.0) + jnp.where(rows == sb, gt[1:2], 0.0)).astype(BF16)
        acc_ref[...] += _dot_tn(wgt, es_ref[...])

    @pl.when((flags & 4) != 0)
    def _():
        o_ref[...] = _rms(acc_ref[...], fg_ref[...]) if final_norm else acc_ref[...]


def _work_lists(cend, t, tc, ts, ne):
    nchunk = t // tc
    nt_max = (2 * t) // ts + ne
    w_max = nt_max + ne * nchunk
    cend = cend.astype(I32)
    cstart = jnp.concatenate([jnp.zeros((1, ne), I32), cend[:-1]], axis=0)
    cnt = cend[-1]
    tiles = (cnt + ts - 1) // ts
    tile_end = jnp.cumsum(tiles)
    off = (tile_end - tiles) * ts
    n_tiles = tile_end[-1]
    tile_ids = jnp.arange(nt_max, dtype=I32)

    def count_le(sorted_vals, queries):
        return jnp.sum(sorted_vals[None, :] <= queries[:, None], axis=1).astype(I32)

    te = jnp.minimum(count_le(tile_end, tile_ids), ne - 1)

    def flatten(counts, w_total):
        ends = jnp.cumsum(counts)
        total = ends[-1]
        wi = jnp.clip(jnp.arange(w_total, dtype=I32), 0, jnp.maximum(total - 1, 0))
        owner = jnp.minimum(count_le(ends, wi), counts.shape[0] - 1)
        local = wi - (ends[owner] - counts[owner])
        valid = jnp.arange(w_total, dtype=I32) < total
        return owner, local, valid

    k0 = tile_ids * ts - off[te]
    k1 = jnp.minimum(k0 + ts, cnt[te])
    ce_t = cend[:, te]
    cs_t = cstart[:, te]
    c_lo = jnp.sum(ce_t <= k0[None, :], axis=0).astype(I32)
    c_hi = (nchunk - 1 - jnp.sum(cs_t >= k1[None, :], axis=0)).astype(I32)
    active = tile_ids < n_tiles
    n_items = jnp.where(active, c_hi - c_lo + 1, 0)
    owner, local, valid = flatten(n_items, w_max)
    f_tile, f_chunk = owner, jnp.clip(c_lo[owner] + local, 0, nchunk - 1)
    first = local == 0
    last = local == n_items[owner] - 1
    f_flags = (valid * (1 + 2 * first + 4 * last)).astype(I32)

    has = cend > cstart
    s_lo = (off[None, :] + cstart) // ts
    s_hi = (off[None, :] + cend - 1) // ts
    n_ce = jnp.where(has, s_hi - s_lo + 1, 0).reshape(-1)
    owner, local, valid = flatten(n_ce, w_max)
    c_chunk = owner // ne
    c_tile = jnp.clip(s_lo.reshape(-1)[owner] + local, 0, nt_max - 1)
    per_chunk = jnp.sum(n_ce.reshape(nchunk, ne), axis=1)
    chunk_end = jnp.cumsum(per_chunk)
    wi = jnp.clip(jnp.arange(w_max, dtype=I32), 0, jnp.maximum(chunk_end[-1] - 1, 0))
    first = wi == (chunk_end - per_chunk)[c_chunk]
    last = wi == chunk_end[c_chunk] - 1
    c_flags = (valid * (1 + 2 * first + 4 * last)).astype(I32)
    return (f_tile.astype(I32), f_chunk.astype(I32), f_flags, te, off.astype(I32),
            c_chunk.astype(I32), c_tile.astype(I32), c_flags, nt_max, w_max)


def moe(x, g, w_router, b_router, wg, wu, wd, layer, final_g=None, tc=MOE_CHUNK, ts=MOE_SLOT_TILE):
    t, d = x.shape
    _, ne, _, f = wg.shape
    tc = min(tc, t)
    h, idx, gate, cend = moe_route(x, g, w_router, b_router, tc)
    (f_tile, f_chunk, f_flags, te, off, c_chunk, c_tile, c_flags, nt_max, w_max) = _work_lists(
        cend[:, :, 0], t, tc, ts, ne)

    sorted_out = pl.pallas_call(
        functools.partial(_moe_ffn_kernel, ts=ts, ne=ne),
        grid_spec=pltpu.PrefetchScalarGridSpec(
            num_scalar_prefetch=5,
            grid=(w_max,),
            in_specs=[
                pl.BlockSpec((ne, tc), lambda w, wt, wc, wf, te_, of: (0, wc[w])),
                pl.BlockSpec((tc, d), lambda w, wt, wc, wf, te_, of: (wc[w], 0)),
                pl.BlockSpec((None, None, d, f), lambda w, wt, wc, wf, te_, of: (layer, te_[wt[w]], 0, 0)),
                pl.BlockSpec((None, None, d, f), lambda w, wt, wc, wf, te_, of: (layer, te_[wt[w]], 0, 0)),
                pl.BlockSpec((None, None, f, d), lambda w, wt, wc, wf, te_, of: (layer, te_[wt[w]], 0, 0)),
            ],
            out_specs=pl.BlockSpec((ts, d), lambda w, wt, wc, wf, te_, of: (wt[w], 0)),
            scratch_shapes=[pltpu.VMEM((ts, d), F32)],
        ),
        out_shape=jax.ShapeDtypeStruct((nt_max * ts, d), BF16),
        compiler_params=_params("arbitrary"),
        name="moe_ffn",
    )(f_tile, f_chunk, f_flags, te, off, idx, h, wg, wu, wd)

    return pl.pallas_call(
        functools.partial(_moe_combine_kernel, ts=ts, ne=ne, final_norm=final_g is not None),
        grid_spec=pltpu.PrefetchScalarGridSpec(
            num_scalar_prefetch=4,
            grid=(w_max,),
            in_specs=[
                pl.BlockSpec((ne, tc), lambda w, wc, wt, wf, of: (0, wc[w])),
                pl.BlockSpec((ne, tc), lambda w, wc, wt, wf, of: (0, wc[w])),
                pl.BlockSpec((tc, d), lambda w, wc, wt, wf, of: (wc[w], 0)),
                pl.BlockSpec((ts, d), lambda w, wc, wt, wf, of: (wt[w], 0)),
            ] + ([pl.BlockSpec((1, d), lambda w, wc, wt, wf, of: (0, 0))] if final_g is not None else []),
            out_specs=pl.BlockSpec((tc, d), lambda w, wc, wt, wf, of: (wc[w], 0)),
            scratch_shapes=[pltpu.VMEM((tc, d), F32)],
        ),
        out_shape=jax.ShapeDtypeStruct((t, d), F32),
        compiler_params=_params("arbitrary"),
        name="moe_combine",
    )(c_chunk, c_tile, c_flags, off, idx, gate, x, sorted_out,
      *([final_g.reshape(1, d)] if final_g is not None else []))


def _lambda_init(layer):
    return 0.8 - 0.6 * math.exp(-0.3 * layer)


def kernel(x_prompt, x_sample, state_ssm, state_conv, cache_k, cache_v, page_table, norm_mix_g, norm_ffn_g, norm_kv_g, norm_final_g, ssm_w_in, ssm_conv_w, ssm_conv_b, ssm_dt_bias, ssm_a_log, ssm_d, ssm_norm_g, ssm_w_out, kv_w_k, kv_w_v, att_w_q, att_lam_q1, att_lam_k1, att_lam_q2, att_lam_k2, att_subln_g, att_w_o, ffn_w_gate, ffn_w_up, ffn_w_down, moe_w_router, moe_b_router, moe_w_gate, moe_w_up, moe_w_down):
    depth, d_model = norm_mix_g.shape
    n_a = ssm_w_in.shape[0]
    heads_ssm = ssm_a_log.shape[1]
    d_inner = heads_ssm * SSM_HEAD_DIM
    conv_dim = ssm_conv_w.shape[2]
    att_dv = att_subln_g.shape[1]
    att_heads = kv_w_v.shape[1] // att_dv
    slopes = 2.0 ** (-8.0 * jnp.arange(1, att_heads + 1, dtype=F32) / att_heads)

    def per_layer(w):
        stack = w.astype(BF16)
        return [(stack, i) for i in range(w.shape[0])]

    in_dim = ssm_w_in.shape[2]
    w_in = per_layer(jnp.pad(ssm_w_in, ((0, 0), (0, 0), (0, d_inner + conv_dim + LANE - in_dim))))
    w_out = per_layer(ssm_w_out)
    w_k, w_v = kv_w_k.astype(BF16), kv_w_v.astype(BF16)
    w_q, w_o = per_layer(att_w_q), per_layer(att_w_o)
    f_gate, f_up, f_down = per_layer(ffn_w_gate), per_layer(ffn_w_up), per_layer(ffn_w_down)
    m_gate, m_up, m_down = moe_w_gate.astype(BF16), moe_w_up.astype(BF16), moe_w_down.astype(BF16)
    lam_vecs = jnp.stack([att_lam_q1, att_lam_k1, att_lam_q2, att_lam_k2], axis=1)

    def trunk(x3, ssm0, conv0, past):
        b, l, _ = x3.shape
        x = x3.reshape(b * l, d_model)
        states = None
        k_new = v_new = k_att = v_att = None
        for layer in range(depth):
            if layer < n_a:
                ssd_w = (ssm_conv_w[layer], ssm_conv_b[layer], ssm_dt_bias[layer], ssm_a_log[layer],
                         ssm_d[layer], ssm_norm_g[layer])
                if l % SSD_CHUNK == 0:
                    xo, *states = ssd_mixer(x.reshape(b, l, d_model), layer, n_a, conv0, ssm0, states, *ssd_w,
                                            SSD_CHUNK, fuse=(norm_mix_g[layer], w_in[layer], w_out[layer]))
                    x = xo.reshape(b * l, d_model)
                else:
                    zx = linear(x, w_in[layer], g=norm_mix_g[layer]).reshape(b, l, -1)
                    y, *states = ssd_mixer(zx, layer, n_a, conv0, ssm0, states, *ssd_w, l)
                    x = linear(y.reshape(b * l, d_inner), w_out[layer], residual=x)
            else:
                j = layer - n_a
                lam_init = _lambda_init(layer)
                if past is None:
                    qt = linear(x, w_q[j], g=norm_mix_g[layer], scale=ATT_HEAD_DIM ** -0.5, outs=("bf16_t",),
                                tm=ATT_TILE)
                    o = attention_prompt(qt.reshape(b, l // ATT_TILE, -1, ATT_TILE), k_att, v_att,
                                         lam_vecs[j], att_subln_g[j], slopes, lam_init)
                else:
                    q = linear(x, w_q[j], g=norm_mix_g[layer])
                    o = attention_decode(q.reshape(b, l, -1), k_att, v_att, past[0], past[1], page_table,
                                         lam_vecs[j], att_subln_g[j], slopes, lam_init)
                x = linear(o.reshape(b * l, -1), w_o[j], residual=x)
            i = layer // 2
            if layer % 2 == 0:
                x = ffn(x, norm_ffn_g[layer], f_gate[i], f_up[i], f_down[i])
            else:
                x = moe(x, norm_ffn_g[layer], moe_w_router[i], moe_b_router[i], m_gate, m_up, m_down, i,
                        final_g=norm_final_g if layer == depth - 1 else None)
            if layer == n_a - 1:
                if past is None:
                    assert l % ATT_TILE == 0
                    k_new, k_att = linear(x, w_k, g=norm_kv_g, outs=("f32", "bf16"))
                    v_new, v_att = linear(x, w_v, g=norm_kv_g, outs=("f32", "bf16_t"), tm=ATT_TILE)
                    k_att = k_att.reshape(b, l, -1)
                    v_att = v_att.reshape(b, l // ATT_TILE, -1, ATT_TILE)
                else:
                    k_att = k_new = linear(x, w_k, g=norm_kv_g).reshape(b, l, -1)
                    v_att = v_new = linear(x, w_v, g=norm_kv_g).reshape(b, l, -1)
        y = (x if depth % 2 == 0 else rmsnorm(x, norm_final_g)).reshape(b, l, d_model)
        return (y, states[0].reshape(n_a, b, heads_ssm, SSM_HEAD_DIM, SSM_STATE), states[1],
                k_new.reshape(b, l, att_heads, att_dv), v_new.reshape(b, l, att_heads, att_dv))

    y_p, ssm_p, conv_p, k_p, v_p = trunk(x_prompt, None, None, None)
    bs = x_sample.shape[0]
    y_s, ssm_s, conv_s, k_s, v_s = trunk(x_sample, state_ssm.reshape(n_a, bs, d_inner, SSM_STATE), state_conv,
                                         (cache_k, cache_v))
    return (y_p, y_s, ssm_p, conv_p, k_p, v_p, ssm_s, conv_s, k_s, v_s)
```
